```python
import jax, jax.numpy as jnp
from jax import lax
import numpy as np

D_MODEL = 2048
BATCH = 32
SEQ = 256
DEPTH = 1
DEC_BATCH = 8
DEC_SEQ = 1024
PAST_LEN = 256

GRID_W = 64
H_MLA = 16
NOPE_DIM = 128
ROPE_DIM = 64
QK_HEAD = NOPE_DIM + ROPE_DIM
V_HEAD = 128
KV_RANK = 512
ROPE_THETA = 10000.0
AXIS_DIM = ROPE_DIM // 2
AXIS_PAIRS = AXIS_DIM // 2
Q_BLOCK = 128
MLA_WIDTH = H_MLA * V_HEAD
Q_DIM = H_MLA * QK_HEAD
HEAD_RWKV = 64
H_RWKV = D_MODEL // HEAD_RWKV
R_DIM = H_RWKV * HEAD_RWKV
W_LORA = 64
A_LORA = 64
G_LORA = 128
CONV_W = 3
N_DIR = 2
LNX_EPS = 64e-5
D_FF = 4 * D_MODEL
EPS = 1e-6
IN_SIZES = (Q_DIM, KV_RANK, ROPE_DIM, 3 * R_DIM, N_DIR * W_LORA, N_DIR * A_LORA, G_LORA, 2 * D_MODEL)
IN_DIM = Q_DIM + KV_RANK + ROPE_DIM + 3 * R_DIM + N_DIR * W_LORA + N_DIR * A_LORA + G_LORA + 2 * D_MODEL

kernel_name = 'hybrid_mla_rwkv7_dit_step'


def rms_norm(x, g):
    xf = x.astype(jnp.float32)
    y = xf * lax.rsqrt(jnp.mean(xf * xf, axis=-1, keepdims=True) + EPS)
    return (y * g.astype(jnp.float32)).astype(x.dtype)


def split_cols(z, sizes):
    out, off = [], 0
    for s in sizes:
        out.append(z[..., off:off + s])
        off += s
    return out


def rotate_pairs(x, ang):
    cos = jnp.cos(ang).astype(x.dtype)[None, :, None, :]
    sin = jnp.sin(ang).astype(x.dtype)[None, :, None, :]
    x1, x2 = x[..., :AXIS_PAIRS], x[..., AXIS_PAIRS:]
    return jnp.concatenate([x1 * cos - x2 * sin, x2 * cos + x1 * sin], axis=-1)


def axial_rope(x):
    T = x.shape[1]
    rows = T // GRID_W
    row = jnp.repeat(jnp.arange(rows, dtype=jnp.float32), GRID_W)
    col = jnp.tile(jnp.arange(GRID_W, dtype=jnp.float32), rows)
    inv_freq = jnp.power(ROPE_THETA, -jnp.arange(AXIS_PAIRS, dtype=jnp.float32) / AXIS_PAIRS)
    x_nope = x[..., :NOPE_DIM]
    x_row = x[..., NOPE_DIM:NOPE_DIM + AXIS_DIM]
    x_col = x[..., NOPE_DIM + AXIS_DIM:]
    return jnp.concatenate([x_nope, rotate_pairs(x_row, row[:, None] * inv_freq),
                            rotate_pairs(x_col, col[:, None] * inv_freq)], axis=-1)


def mla_keys_values(ckv, kr, p):
    B, L, _ = ckv.shape
    kv = (rms_norm(ckv, p['kv_norm']) @ p['w_kv_up']).reshape(B, L, H_MLA, NOPE_DIM + V_HEAD)
    k_rope = jnp.broadcast_to(kr[:, :, None, :], (B, L, H_MLA, ROPE_DIM))
    k = rms_norm(jnp.concatenate([kv[..., :NOPE_DIM], k_rope], axis=-1), p['k_norm'])
    return k, kv[..., NOPE_DIM:]


def block_attention(q, k, v):
    B, T, H, _ = q.shape
    nb = T // Q_BLOCK
    qb = jnp.moveaxis(q.reshape(B, nb, Q_BLOCK, H, QK_HEAD), 1, 0)
    scale = QK_HEAD ** -0.5

    def one_block(q_blk):
        s = jnp.einsum('bqhd,bkhd->bhqk', q_blk, k).astype(jnp.float32) * scale
        pr = jax.nn.softmax(s, axis=-1).astype(v.dtype)
        return jnp.einsum('bhqk,bkhd->bqhd', pr, v)

    o = lax.map(one_block, qb)
    return jnp.moveaxis(o, 0, 1).reshape(B, T, H * V_HEAD)


def centred_conv(x, w):
    xp = jnp.pad(x, ((0, 0), (1, 1), (0, 0)))
    return xp[:, :-2] * w[0] + xp[:, 1:-1] * w[1] + xp[:, 2:] * w[2]


def l2_normalize(x):
    xf = x.astype(jnp.float32)
    return (xf * lax.rsqrt(jnp.sum(xf * xf, axis=-1, keepdims=True) + 1e-12)).astype(x.dtype)


def rwkv7_scan(r, decay, k, v, kk, a, s0, reverse):
    seq = tuple(jnp.swapaxes(t.astype(jnp.float32), 0, 1) for t in (r, decay, k, v, kk, a))

    def step(S, inp):
        r_t, w_t, k_t, v_t, kk_t, a_t = inp
        s_kk = jnp.einsum('bhvk,bhk->bhv', S, kk_t)
        S = (S * w_t[:, :, None, :]
             - jnp.einsum('bhv,bhk->bhvk', s_kk, kk_t * a_t)
             + jnp.einsum('bhv,bhk->bhvk', v_t, k_t))
        return S, jnp.einsum('bhvk,bhk->bhv', S, r_t)

    s_fin, ys = lax.scan(step, s0.astype(jnp.float32), seq, reverse=reverse)
    return jnp.swapaxes(ys, 0, 1), s_fin


def head_group_norm(y, w, b):
    mu = jnp.mean(y, axis=-1, keepdims=True)
    var = jnp.mean(jnp.square(y - mu), axis=-1, keepdims=True)
    yn = ((y - mu) * lax.rsqrt(var + LNX_EPS)).reshape(y.shape[:2] + (R_DIM,))
    return yn * w.astype(jnp.float32) + b.astype(jnp.float32)


def rwkv7_branch(z_rkv, z_wd, z_ad, z_gd, p, s0):
    B, T, _ = z_rkv.shape
    rkv = centred_conv(z_rkv, p['conv_rkv'])
    r, k, v = [t.reshape(B, T, H_RWKV, HEAD_RWKV) for t in jnp.split(rkv, 3, axis=-1)]
    kk = l2_normalize(k * p['k_k'].reshape(H_RWKV, HEAD_RWKV))
    g = jax.nn.sigmoid(z_gd) @ p['g_up']
    wd = z_wd.reshape(B, T, N_DIR, W_LORA)
    ad = z_ad.reshape(B, T, N_DIR, A_LORA)
    k_a = p['k_a'].reshape(H_RWKV, HEAD_RWKV)
    ys, bonuses, finals = [], [], []
    for d in range(N_DIR):
        w_log = -jax.nn.softplus(-(p['w0'][d] + jnp.tanh(wd[:, :, d]) @ p['w_up'][d])) - 0.5
        decay = jnp.exp(-jnp.exp(w_log.astype(jnp.float32))).reshape(B, T, H_RWKV, HEAD_RWKV)
        a = jax.nn.sigmoid(p['a0'][d] + ad[:, :, d] @ p['a_up'][d]).reshape(B, T, H_RWKV, HEAD_RWKV)
        k_d = k * (1.0 + (a - 1.0) * k_a)
        y, s_fin = rwkv7_scan(r, decay, k_d, v, kk, a, s0[:, d], reverse=(d == 1))
        ys.append(y)
        bonuses.append(jnp.sum(r * k_d * p['r_k'], axis=-1, keepdims=True) * v)
        finals.append(s_fin)
    o = head_group_norm(ys[0] + ys[1], p['lnx_w'], p['lnx_b']).astype(z_rkv.dtype)
    o = o + (bonuses[0] + bonuses[1]).reshape(B, T, R_DIM)
    return o * g, jnp.stack(finals, axis=1)


def trunk_layer(x, cond, p, cache):
    B, T, _ = x.shape
    mod = (jax.nn.silu(cond) @ p['w_ada'] + p['b_ada']).reshape(-1, 1, 6 * D_MODEL)
    shift1, scale1, gate1, shift2, scale2, gate2 = jnp.split(mod, 6, axis=-1)
    h = rms_norm(x, p['norm1']) * (1.0 + scale1) + shift1
    z_q, z_ckv, z_kr, z_rkv, z_wd, z_ad, z_gd, z_gate = split_cols(h @ p['w_in'], IN_SIZES)
    q = rms_norm(z_q.reshape(B, T, H_MLA, QK_HEAD), p['q_norm'])
    k, v = mla_keys_values(z_ckv, z_kr, p)
    if cache is None:
        s0 = jnp.zeros((B, N_DIR, H_RWKV, HEAD_RWKV, HEAD_RWKV), jnp.float32)
    else:
        ckv_ctx, kr_ctx, s0 = cache
        q = axial_rope(q)
        k = axial_rope(k)
        k_ctx, v_ctx = mla_keys_values(ckv_ctx, kr_ctx, p)
        k = jnp.concatenate([k, k_ctx], axis=1)
        v = jnp.concatenate([v, v_ctx], axis=1)
    o_mla = block_attention(q, k, v)
    o_rwkv, s_final = rwkv7_branch(z_rkv, z_wd, z_ad, z_gd, p, s0)
    gates = jax.nn.sigmoid(z_gate)
    merged = (gates[..., :D_MODEL] * (o_mla @ p['w_br_mla'])
              + gates[..., D_MODEL:] * (o_rwkv @ p['w_br_rwkv']))
    x = x + gate1 * (merged @ p['w_out'])
    h2 = rms_norm(x, p['norm2']) * (1.0 + scale2) + shift2
    x = x + gate2 * (jnp.square(jax.nn.relu(h2 @ p['w_ff_in'])) @ p['w_ff_out'])
    return x, (z_ckv, z_kr, s_final)


def setup_inputs(seed: int = 0) -> dict:
    key = jax.random.key(seed)
    ks = iter(jax.random.split(key, 40))

    def nrm(shape, scale):
        return scale * jax.random.normal(next(ks), shape, jnp.float32)

    L = DEPTH
    R = R_DIM
    conv = nrm((L, CONV_W, 3 * R), 0.2).at[:, CONV_W // 2].add(1.0)
    return {
        'x_prompt': nrm((BATCH, SEQ, D_MODEL), 1.0),
        'x_sample': nrm((DEC_BATCH, DEC_SEQ, D_MODEL), 1.0),
        'cache_mla_ckv': nrm((DEC_BATCH, L, PAST_LEN, KV_RANK), 1.0),
        'cache_mla_kr': nrm((DEC_BATCH, L, PAST_LEN, ROPE_DIM), 1.0),
        'state_rwkv': nrm((DEC_BATCH, L, N_DIR, H_RWKV, HEAD_RWKV, HEAD_RWKV), 1.0),
        'c': nrm((DEC_BATCH, D_MODEL), 1.0),
        'c_ctx': nrm((D_MODEL,), 1.0),
        'norm1': 1.0 + nrm((L, D_MODEL), 0.05),
        'w_ada': nrm((L, D_MODEL, 6 * D_MODEL), 0.01),
        'b_ada': nrm((L, 6 * D_MODEL), 0.1),
        'w_in': nrm((L, D_MODEL, IN_DIM), D_MODEL ** -0.5),
        'q_norm': 1.0 + nrm((L, QK_HEAD), 0.05),
        'kv_norm': 1.0 + nrm((L, KV_RANK), 0.05),
        'w_kv_up': nrm((L, KV_RANK, H_MLA * (NOPE_DIM + V_HEAD)), KV_RANK ** -0.5),
        'k_norm': 1.0 + nrm((L, QK_HEAD), 0.05),
        'conv_rkv': conv,
        'k_k': 0.85 + nrm((L, R), 0.05),
        'k_a': 1.0 + nrm((L, R), 0.05),
        'r_k': nrm((L, H_RWKV, HEAD_RWKV), 0.1),
        'w0': -1.5 + nrm((L, N_DIR, R), 0.5),
        'w_up': nrm((L, N_DIR, W_LORA, R), 0.5 * W_LORA ** -0.5),
        'a0': nrm((L, N_DIR, R), 0.1),
        'a_up': nrm((L, N_DIR, A_LORA, R), A_LORA ** -0.5),
        'g_up': nrm((L, G_LORA, R), G_LORA ** -0.5),
        'lnx_w': 1.0 + nrm((L, R), 0.05),
        'lnx_b': nrm((L, R), 0.01),
        'w_br_mla': nrm((L, MLA_WIDTH, D_MODEL), MLA_WIDTH ** -0.5),
        'w_br_rwkv': nrm((L, R, D_MODEL), R ** -0.5),
        'w_out': nrm((L, D_MODEL, D_MODEL), D_MODEL ** -0.5),
        'norm2': 1.0 + nrm((L, D_MODEL), 0.05),
        'w_ff_in': nrm((L, D_MODEL, D_FF), D_MODEL ** -0.5),
        'w_ff_out': nrm((L, D_FF, D_MODEL), D_FF ** -0.5),
    }


def reference(x_prompt, x_sample, cache_mla_ckv, cache_mla_kr, state_rwkv, c, c_ctx,
              norm1, w_ada, b_ada, w_in, q_norm, kv_norm, w_kv_up, k_norm, conv_rkv,
              k_k, k_a, r_k, w0, w_up, a0, a_up, g_up, lnx_w, lnx_b,
              w_br_mla, w_br_rwkv, w_out, norm2, w_ff_in, w_ff_out):
    x_p = x_prompt
    x_s = x_sample
    ckv_list, kr_list, st_list = [], [], []
    for l in range(DEPTH):
        p = {
            'norm1': norm1[l], 'w_ada': w_ada[l], 'b_ada': b_ada[l], 'w_in': w_in[l],
            'q_norm': q_norm[l], 'kv_norm': kv_norm[l], 'w_kv_up': w_kv_up[l], 'k_norm': k_norm[l],
            'conv_rkv': conv_rkv[l], 'k_k': k_k[l], 'k_a': k_a[l], 'r_k': r_k[l],
            'w0': w0[l], 'w_up': w_up[l], 'a0': a0[l], 'a_up': a_up[l], 'g_up': g_up[l],
            'lnx_w': lnx_w[l], 'lnx_b': lnx_b[l], 'w_br_mla': w_br_mla[l], 'w_br_rwkv': w_br_rwkv[l],
            'w_out': w_out[l], 'norm2': norm2[l], 'w_ff_in': w_ff_in[l], 'w_ff_out': w_ff_out[l],
        }
        x_p, (ckv_l, kr_l, st_l) = trunk_layer(x_p, c_ctx, p, None)
        ckv_list.append(ckv_l)
        kr_list.append(kr_l)
        st_list.append(st_l)
        x_s, _ = trunk_layer(x_s, c, p, (cache_mla_ckv[:, l], cache_mla_kr[:, l], state_rwkv[:, l]))
    new_cache_mla_ckv = jnp.stack(ckv_list, axis=1)
    new_cache_mla_kr = jnp.stack(kr_list, axis=1)
    new_state_rwkv = jnp.stack(st_list, axis=1)
    return (x_p, x_s, new_cache_mla_ckv, new_cache_mla_kr, new_state_rwkv)
```

```python
import functools

import jax
import jax.numpy as jnp
from jax import lax
from jax.experimental import pallas as pl
from jax.experimental.pallas import tpu as pltpu

D_MODEL = 2048
DEPTH = 1
GRID_W = 64
H_MLA = 16
NOPE_DIM = 128
ROPE_DIM = 64
QK_HEAD = NOPE_DIM + ROPE_DIM
V_HEAD = 128
KV_RANK = 512
ROPE_THETA = 10000.0
AXIS_DIM = ROPE_DIM // 2
AXIS_PAIRS = AXIS_DIM // 2
MLA_WIDTH = H_MLA * V_HEAD
Q_DIM = H_MLA * QK_HEAD
HEAD_RWKV = 64
H_RWKV = D_MODEL // HEAD_RWKV
R_DIM = H_RWKV * HEAD_RWKV
W_LORA = 64
A_LORA = 64
G_LORA = 128
N_DIR = 2
LNX_EPS = 64e-5
D_FF = 4 * D_MODEL
EPS = 1e-6
IN_SIZES = (Q_DIM, KV_RANK, ROPE_DIM, 3 * R_DIM, N_DIR * W_LORA, N_DIR * A_LORA, G_LORA, 2 * D_MODEL)
IN_DIM = sum(IN_SIZES)

F32 = jnp.float32
BF16 = jnp.bfloat16
VMEM_LIMIT = 56 * 1024 * 1024
SCAN_CHUNK = 64
SCAN_HEADS = 4


def _params(sem):
    return pltpu.CompilerParams(dimension_semantics=sem, vmem_limit_bytes=VMEM_LIMIT)


def _mm_kernel(a_ref, b_ref, o_ref):
    o_ref[...] = jnp.dot(a_ref[...].astype(BF16), b_ref[...],
                         preferred_element_type=F32).astype(o_ref.dtype)


def _matmul(a, b, *, tm=512, tn=512, out_dtype=F32):
    M, K = a.shape
    N = b.shape[1]
    tm = min(tm, M)
    tn = min(tn, N)
    assert M % tm == 0 and N % tn == 0, (M, N, tm, tn)
    return pl.pallas_call(
        _mm_kernel,
        grid=(M // tm, N // tn),
        in_specs=[pl.BlockSpec((tm, K), lambda i, j: (i, 0)),
                  pl.BlockSpec((K, tn), lambda i, j: (0, j))],
        out_specs=pl.BlockSpec((tm, tn), lambda i, j: (i, j)),
        out_shape=jax.ShapeDtypeStruct((M, N), out_dtype),
        compiler_params=_params(("parallel", "arbitrary")),
        name="matmul",
    )(a, b.astype(BF16))


def _modulated_norm(x, g, scale, shift):
    ms = jnp.mean(x * x, axis=-1, keepdims=True)
    return (x * lax.rsqrt(ms + EPS) * g) * (1.0 + scale) + shift


def _norm_mm_kernel(x_ref, g_ref, sc_ref, sh_ref, w_ref, o_ref, h_scr):
    @pl.when(pl.program_id(1) == 0)
    def _():
        h = _modulated_norm(x_ref[...], g_ref[...], sc_ref[0], sh_ref[0])
        h_scr[...] = h.astype(BF16)

    o_ref[...] = jnp.dot(h_scr[...], w_ref[...], preferred_element_type=F32)


def _mod_index(nb, rows_per_batch, tm):
    if nb == 1:
        return lambda i, j: (0, 0, 0)
    assert rows_per_batch % tm == 0
    return lambda i, j: (i * tm // rows_per_batch, 0, 0)


def _norm_matmul(x, g, scale, shift, w, rows_per_batch, *, tm=1024, tn=512):
    M, K = x.shape
    N = w.shape[1]
    nb = scale.shape[0]
    tm = min(tm, M, rows_per_batch) if nb > 1 else min(tm, M)
    assert M % tm == 0 and N % tn == 0
    midx = _mod_index(nb, rows_per_batch, tm)
    return pl.pallas_call(
        _norm_mm_kernel,
        grid=(M // tm, N // tn),
        in_specs=[pl.BlockSpec((tm, K), lambda i, j: (i, 0)),
                  pl.BlockSpec((1, K), lambda i, j: (0, 0)),
                  pl.BlockSpec((1, 1, K), midx),
                  pl.BlockSpec((1, 1, K), midx),
                  pl.BlockSpec((K, tn), lambda i, j: (0, j))],
        out_specs=pl.BlockSpec((tm, tn), lambda i, j: (i, j)),
        out_shape=jax.ShapeDtypeStruct((M, N), F32),
        scratch_shapes=[pltpu.VMEM((tm, K), BF16)],
        compiler_params=_params(("parallel", "arbitrary")),
        name="norm_matmul",
    )(x, g.reshape(1, K), scale, shift, w)


def _ffn_kernel(x_ref, g_ref, sc_ref, sh_ref, gt_ref, w1_ref, w2_ref, o_ref, h_scr, acc_scr):
    f = pl.program_id(1)

    @pl.when(f == 0)
    def _():
        h = _modulated_norm(x_ref[...], g_ref[...], sc_ref[0], sh_ref[0])
        h_scr[...] = h.astype(BF16)
        acc_scr[...] = jnp.zeros_like(acc_scr)

    u = jnp.dot(h_scr[...], w1_ref[...], preferred_element_type=F32)
    u = jnp.square(jnp.maximum(u, 0.0))
    acc_scr[...] += jnp.dot(u.astype(BF16), w2_ref[...], preferred_element_type=F32)

    @pl.when(f == pl.num_programs(1) - 1)
    def _():
        o_ref[...] = x_ref[...] + gt_ref[0] * acc_scr[...]


def _ffn(x, g, scale, shift, gate, w1, w2, rows_per_batch, *, tm=512, tf=512):
    M, K = x.shape
    F = w1.shape[1]
    nb = scale.shape[0]
    tm = min(tm, M, rows_per_batch) if nb > 1 else min(tm, M)
    assert M % tm == 0 and F % tf == 0
    midx = _mod_index(nb, rows_per_batch, tm)
    return pl.pallas_call(
        _ffn_kernel,
        grid=(M // tm, F // tf),
        in_specs=[pl.BlockSpec((tm, K), lambda i, j: (i, 0)),
                  pl.BlockSpec((1, K), lambda i, j: (0, 0)),
                  pl.BlockSpec((1, 1, K), midx),
                  pl.BlockSpec((1, 1, K), midx),
                  pl.BlockSpec((1, 1, K), midx),
                  pl.BlockSpec((K, tf), lambda i, j: (0, j)),
                  pl.BlockSpec((tf, K), lambda i, j: (j, 0))],
        out_specs=pl.BlockSpec((tm, K), lambda i, j: (i, 0)),
        out_shape=jax.ShapeDtypeStruct((M, K), F32),
        scratch_shapes=[pltpu.VMEM((tm, K), BF16), pltpu.VMEM((tm, K), F32)],
        compiler_params=_params(("parallel", "arbitrary")),
        name="ffn",
    )(x, g.reshape(1, K), scale, shift, gate, w1, w2)


def _attn_kernel(q_ref, k_ref, v_ref, o_ref, *, scale):
    q = q_ref[0, 0]
    k = k_ref[0, 0]
    s = lax.dot_general(q, k, (((1,), (1,)), ((), ())), preferred_element_type=F32) * scale
    m = jnp.max(s, axis=-1, keepdims=True)
    p = jnp.exp(s - m)
    l = jnp.sum(p, axis=-1, keepdims=True)
    o = jnp.dot(p.astype(BF16), v_ref[0, 0], preferred_element_type=F32)
    o_ref[0] = o / l


def _attention(q, k, v, *, tq=256):
    B, H, T, Dk = q.shape
    S = k.shape[2]
    Dv = v.shape[3]
    tq = min(tq, T)
    return pl.pallas_call(
        functools.partial(_attn_kernel, scale=Dk ** -0.5),
        grid=(B, H, T // tq),
        in_specs=[pl.BlockSpec((1, 1, tq, Dk), lambda b, h, i: (b, h, i, 0)),
                  pl.BlockSpec((1, 1, S, Dk), lambda b, h, i: (b, h, 0, 0)),
                  pl.BlockSpec((1, 1, S, Dv), lambda b, h, i: (b, h, 0, 0))],
        out_specs=pl.BlockSpec((1, tq, Dv), lambda b, h, i: (b, i, h)),
        out_shape=jax.ShapeDtypeStruct((B, T, H * Dv), F32),
        compiler_params=_params(("parallel", "parallel", "arbitrary")),
        name="attention",
    )(q, k, v)


def _dot(a, b, dims):
    return lax.dot_general(a, b, (dims, ((), ())), precision=lax.Precision.HIGHEST,
                           preferred_element_type=F32)


_NN = ((1,), (0,))
_NT = ((1,), (1,))
_TN = ((0,), (0,))


def _scan_kernel(r_ref, v_ref, kk_ref, k_ref, a_ref, ld_ref, s0_ref, y_ref, sf_ref, s_scr,
                 *, chunk, heads):
    d = pl.program_id(0)
    c = pl.program_id(3)
    N = HEAD_RWKV

    @pl.when(c == 0)
    def _():
        s_scr[...] = s0_ref[0, 0]

    sgn = 1 - 2 * d
    ii = lax.broadcasted_iota(jnp.int32, (chunk, chunk), 0)
    jj = lax.broadcasted_iota(jnp.int32, (chunk, chunk), 1)
    rel = (ii - jj) * sgn
    incl = rel >= 0
    strict = rel > 0
    eye = (ii == jj).astype(F32)

    ld = ld_ref[0, 0]
    cum = _dot(incl.astype(F32), ld, _NN)
    tot = jnp.sum(ld, axis=0, keepdims=True)
    e_cum = jnp.exp(cum)
    e_neg = jnp.exp(-cum)
    e_end = jnp.exp(tot - cum)
    kk = kk_ref[0]
    a = a_ref[0, 0]
    b = kk * a
    k = k_ref[0, 0]
    a_t = -kk * jnp.exp(cum - ld)
    r_t = r_ref[0] * e_cum
    b_t = b * e_neg
    k_t = k * e_neg
    b_d = b * e_end
    k_d = k * e_end
    v = v_ref[0]
    e_tot = jnp.exp(tot)

    ys = []
    for h in range(heads):
        sl = slice(h * N, (h + 1) * N)
        S = s_scr[h]
        At, Rt, Bt, Kt, V = a_t[:, sl], r_t[:, sl], b_t[:, sl], k_t[:, sl], v[:, sl]
        a_ab = jnp.where(strict, _dot(At, Bt, _NT), 0.0)
        a_ak = jnp.where(strict, _dot(At, Kt, _NT), 0.0)
        p_rb = jnp.where(incl, _dot(Rt, Bt, _NT), 0.0)
        p_rk = jnp.where(incl, _dot(Rt, Kt, _NT), 0.0)
        w = _dot(At, S, _NT) + _dot(a_ak, V, _NN)
        x = eye + a_ab
        lp = a_ab
        p = 1
        while 2 * p < chunk:
            lp = _dot(lp, lp, _NN)
            x = x + _dot(lp, x, _NN)
            p *= 2
        u = _dot(x, w, _NN)
        ys.append(_dot(Rt, S, _NT) + _dot(p_rb, u, _NN) + _dot(p_rk, V, _NN))
        s_scr[h] = (S * e_tot[:, sl] + _dot(u, b_d[:, sl], _TN) + _dot(V, k_d[:, sl], _TN))
    y_ref[0, 0] = jnp.concatenate(ys, axis=-1)

    @pl.when(c == pl.num_programs(3) - 1)
    def _():
        sf_ref[0, 0] = s_scr[...]


def _rwkv_scan(r, v, kk, k, a, ld, s0):
    B, T, Rd = r.shape
    C = SCAN_CHUNK
    hb = SCAN_HEADS
    nC = T // C
    W = hb * HEAD_RWKV

    def tmap(dd, bb, g, cc):
        return cc + dd * (nC - 1 - 2 * cc)

    shared = pl.BlockSpec((1, C, W), lambda dd, bb, g, cc: (bb, tmap(dd, bb, g, cc), g))
    per_dir = pl.BlockSpec((1, 1, C, W), lambda dd, bb, g, cc: (dd, bb, tmap(dd, bb, g, cc), g))
    state = pl.BlockSpec((1, 1, hb, HEAD_RWKV, HEAD_RWKV), lambda dd, bb, g, cc: (bb, dd, g, 0, 0))
    return pl.pallas_call(
        functools.partial(_scan_kernel, chunk=C, heads=hb),
        grid=(N_DIR, B, H_RWKV // hb, nC),
        in_specs=[shared, shared, shared, per_dir, per_dir, per_dir, state],
        out_specs=[per_dir, state],
        out_shape=[jax.ShapeDtypeStruct((N_DIR, B, T, Rd), F32),
                   jax.ShapeDtypeStruct((B, N_DIR, H_RWKV, HEAD_RWKV, HEAD_RWKV), F32)],
        scratch_shapes=[pltpu.VMEM((hb, HEAD_RWKV, HEAD_RWKV), F32)],
        compiler_params=_params(("parallel", "parallel", "parallel", "arbitrary")),
        name="rwkv_scan",
    )(r, v, kk, k, a, ld, s0)


def _rms_norm(x, g):
    return x * lax.rsqrt(jnp.mean(x * x, axis=-1, keepdims=True) + EPS) * g


def _rotate_pairs(x, ang):
    cos = jnp.cos(ang)[None, :, None, :]
    sin = jnp.sin(ang)[None, :, None, :]
    x1, x2 = x[..., :AXIS_PAIRS], x[..., AXIS_PAIRS:]
    return jnp.concatenate([x1 * cos - x2 * sin, x2 * cos + x1 * sin], axis=-1)


def _axial_rope(x):
    T = x.shape[1]
    rows = T // GRID_W
    row = jnp.repeat(jnp.arange(rows, dtype=F32), GRID_W)
    col = jnp.tile(jnp.arange(GRID_W, dtype=F32), rows)
    inv_freq = jnp.power(ROPE_THETA, -jnp.arange(AXIS_PAIRS, dtype=F32) / AXIS_PAIRS)
    x_nope = x[..., :NOPE_DIM]
    x_row = x[..., NOPE_DIM:NOPE_DIM + AXIS_DIM]
    x_col = x[..., NOPE_DIM + AXIS_DIM:]
    return jnp.concatenate([x_nope, _rotate_pairs(x_row, row[:, None] * inv_freq),
                            _rotate_pairs(x_col, col[:, None] * inv_freq)], axis=-1)


def _keys_values(ckv, kr, p):
    B, L, _ = ckv.shape
    kv = _matmul(_rms_norm(ckv, p['kv_norm']).reshape(B * L, KV_RANK), p['w_kv_up'])
    kv = kv.reshape(B, L, H_MLA, NOPE_DIM + V_HEAD)
    k_rope = jnp.broadcast_to(kr[:, :, None, :], (B, L, H_MLA, ROPE_DIM))
    k = _rms_norm(jnp.concatenate([kv[..., :NOPE_DIM], k_rope], axis=-1), p['k_norm'])
    return k, kv[..., NOPE_DIM:]


def _centred_conv(x, w):
    xp = jnp.pad(x, ((0, 0), (1, 1), (0, 0)))
    return xp[:, :-2] * w[0] + xp[:, 1:-1] * w[1] + xp[:, 2:] * w[2]


def _rwkv_branch(z_rkv, z_wd, z_ad, z_gd, p, s0):
    B, T, _ = z_rkv.shape
    M = B * T
    rkv = _centred_conv(z_rkv, p['conv_rkv'])
    r, k, v = jnp.split(rkv, 3, axis=-1)
    kh = (k * p['k_k']).reshape(B, T, H_RWKV, HEAD_RWKV)
    kk = (kh * lax.rsqrt(jnp.sum(kh * kh, axis=-1, keepdims=True) + 1e-12)).reshape(B, T, R_DIM)
    g = _matmul(jax.nn.sigmoid(z_gd).reshape(M, G_LORA), p['g_up']).reshape(B, T, R_DIM)
    ks, As, lds = [], [], []
    for d in range(N_DIR):
        wd = jnp.tanh(z_wd[..., d * W_LORA:(d + 1) * W_LORA]).reshape(M, W_LORA)
        ad = z_ad[..., d * A_LORA:(d + 1) * A_LORA].reshape(M, A_LORA)
        w_log = -jax.nn.softplus(-(p['w0'][d] + _matmul(wd, p['w_up'][d]))) - 0.5
        lds.append((-jnp.exp(w_log)).reshape(B, T, R_DIM))
        a = jax.nn.sigmoid(p['a0'][d] + _matmul(ad, p['a_up'][d])).reshape(B, T, R_DIM)
        As.append(a)
        ks.append(k * (1.0 + (a - 1.0) * p['k_a']))
    k_d = jnp.stack(ks)
    y, s_fin = _rwkv_scan(r, v, kk, k_d, jnp.stack(As), jnp.stack(lds), s0)
    ysum = (y[0] + y[1]).reshape(B, T, H_RWKV, HEAD_RWKV)
    mu = jnp.mean(ysum, axis=-1, keepdims=True)
    var = jnp.mean(jnp.square(ysum - mu), axis=-1, keepdims=True)
    yn = ((ysum - mu) * lax.rsqrt(var + LNX_EPS)).reshape(B, T, R_DIM)
    o = yn * p['lnx_w'] + p['lnx_b']
    rk = (r[None] * k_d).reshape(N_DIR, B, T, H_RWKV, HEAD_RWKV) * p['r_k']
    bonus = jnp.sum(jnp.sum(rk, axis=-1, keepdims=True), axis=0) * v.reshape(B, T, H_RWKV, HEAD_RWKV)
    o = o + bonus.reshape(B, T, R_DIM)
    return o * g, s_fin


def _trunk_layer(x, mod, p, cache):
    B, T, _ = x.shape
    M = B * T
    nb = mod.shape[0]
    shift1, scale1, gate1, shift2, scale2, gate2 = [
        m.reshape(nb, 1, D_MODEL) for m in jnp.split(mod, 6, axis=-1)]
    xf = x.reshape(M, D_MODEL)
    z = _norm_matmul(xf, p['norm1'], scale1, shift1, p['w_in'], T).reshape(B, T, -1)
    z_q, z_ckv, z_kr, z_rkv, z_wd, z_ad, z_gd, z_gate = [
        z[..., o:o + s] for o, s in zip(p['in_offsets'], IN_SIZES)]
    q = _rms_norm(z_q.reshape(B, T, H_MLA, QK_HEAD), p['q_norm'])
    k, v = _keys_values(z_ckv, z_kr, p)
    if cache is None:
        s0 = jnp.zeros((B, N_DIR, H_RWKV, HEAD_RWKV, HEAD_RWKV), F32)
    else:
        ckv_ctx, kr_ctx, s0 = cache
        q = _axial_rope(q)
        k = _axial_rope(k)
        k_ctx, v_ctx = _keys_values(ckv_ctx, kr_ctx, p)
        k = jnp.concatenate([k, k_ctx], axis=1)
        v = jnp.concatenate([v, v_ctx], axis=1)
    o_mla = _attention(jnp.swapaxes(q, 1, 2).astype(BF16), jnp.swapaxes(k, 1, 2).astype(BF16),
                       jnp.swapaxes(v, 1, 2).astype(BF16))
    o_rwkv, s_final = _rwkv_branch(z_rkv, z_wd, z_ad, z_gd, p, s0)
    gates = jax.nn.sigmoid(z_gate).reshape(M, 2 * D_MODEL)
    merged = (gates[:, :D_MODEL] * _matmul(o_mla.reshape(M, MLA_WIDTH), p['w_br_mla'])
              + gates[:, D_MODEL:] * _matmul(o_rwkv.reshape(M, R_DIM), p['w_br_rwkv']))
    x1 = x + gate1 * _matmul(merged, p['w_out']).reshape(B, T, D_MODEL)
    x2 = _ffn(x1.reshape(M, D_MODEL), p['norm2'], scale2, shift2, gate2,
              p['w_ff_in'], p['w_ff_out'], T).reshape(B, T, D_MODEL)
    return x2, (z_ckv, z_kr, s_final)


def _pad_cols(w, total):
    parts = []
    off = 0
    for s in IN_SIZES:
        blk = w[:, off:off + s]
        pad = (-s) % 128
        if pad:
            blk = jnp.pad(blk, ((0, 0), (0, pad)))
        parts.append(blk)
        off += s
    out = jnp.concatenate(parts, axis=1)
    pad = total - out.shape[1]
    return jnp.pad(out, ((0, 0), (0, pad))) if pad else out


def kernel(x_prompt, x_sample, cache_mla_ckv, cache_mla_kr, state_rwkv, c, c_ctx,
           norm1, w_ada, b_ada, w_in, q_norm, kv_norm, w_kv_up, k_norm, conv_rkv,
           k_k, k_a, r_k, w0, w_up, a0, a_up, g_up, lnx_w, lnx_b,
           w_br_mla, w_br_rwkv, w_out, norm2, w_ff_in, w_ff_out):
    x_p, x_s = x_prompt, x_sample
    n_lat = c.shape[0]
    padded_sizes = [s + (-s) % 128 for s in IN_SIZES]
    in_offsets = [sum(padded_sizes[:i]) for i in range(len(IN_SIZES))]
    in_total = sum(padded_sizes)
    in_total += (-in_total) % 512
    ckv_list, kr_list, st_list = [], [], []
    for l in range(DEPTH):
        p = {
            'norm1': norm1[l], 'q_norm': q_norm[l], 'kv_norm': kv_norm[l], 'k_norm': k_norm[l],
            'conv_rkv': conv_rkv[l], 'k_k': k_k[l], 'k_a': k_a[l], 'r_k': r_k[l],
            'w0': w0[l], 'w_up': w_up[l], 'a0': a0[l], 'a_up': a_up[l], 'g_up': g_up[l],
            'lnx_w': lnx_w[l], 'lnx_b': lnx_b[l], 'norm2': norm2[l],
            'w_in': _pad_cols(w_in[l].astype(BF16), in_total), 'in_offsets': in_offsets,
            'w_kv_up': w_kv_up[l].astype(BF16),
            'w_br_mla': w_br_mla[l].astype(BF16), 'w_br_rwkv': w_br_rwkv[l].astype(BF16),
            'w_out': w_out[l].astype(BF16),
            'w_ff_in': w_ff_in[l].astype(BF16), 'w_ff_out': w_ff_out[l].astype(BF16),
        }
        cond = jnp.concatenate([c, c_ctx[None]], axis=0)
        cond = jnp.pad(jax.nn.silu(cond), ((0, (-cond.shape[0]) % 16), (0, 0)))
        mod = _matmul(cond, w_ada[l], tm=16, tn=1024)[:n_lat + 1] + b_ada[l]
        x_p, (ckv_l, kr_l, st_l) = _trunk_layer(x_p, mod[n_lat:], p, None)
        ckv_list.append(ckv_l)
        kr_list.append(kr_l)
        st_list.append(st_l)
        x_s, _ = _trunk_layer(x_s, mod[:n_lat], p,
                              (cache_mla_ckv[:, l], cache_mla_kr[:, l], state_rwkv[:, l]))
    return (x_p, x_s, jnp.stack(ckv_list, axis=1), jnp.stack(kr_list, axis=1),
            jnp.stack(st_list, axis=1))
```

```python
import functools

import jax
import jax.numpy as jnp
from jax import lax
from jax.experimental import pallas as pl
from jax.experimental.pallas import tpu as pltpu

D_MODEL = 2048
DEPTH = 1
GRID_W = 64
H_MLA = 16
NOPE_DIM = 128
ROPE_DIM = 64
QK_HEAD = NOPE_DIM + ROPE_DIM
V_HEAD = 128
KV_RANK = 512
ROPE_THETA = 10000.0
AXIS_DIM = ROPE_DIM // 2
AXIS_PAIRS = AXIS_DIM // 2
MLA_WIDTH = H_MLA * V_HEAD
Q_DIM = H_MLA * QK_HEAD
HEAD_RWKV = 64
H_RWKV = D_MODEL // HEAD_RWKV
R_DIM = H_RWKV * HEAD_RWKV
W_LORA = 64
A_LORA = 64
G_LORA = 128
N_DIR = 2
LNX_EPS = 64e-5
D_FF = 4 * D_MODEL
EPS = 1e-6
IN_SIZES = (Q_DIM, KV_RANK, ROPE_DIM, 3 * R_DIM, N_DIR * W_LORA, N_DIR * A_LORA, G_LORA, 2 * D_MODEL)
IN_DIM = sum(IN_SIZES)

F32 = jnp.float32
BF16 = jnp.bfloat16
VMEM_LIMIT = 56 * 1024 * 1024
SCAN_CHUNK = 64
SCAN_PAIRS = 16


def _params(sem):
    return pltpu.CompilerParams(dimension_semantics=sem, vmem_limit_bytes=VMEM_LIMIT)


def _mm_kernel(a_ref, b_ref, o_ref):
    o_ref[...] = jnp.dot(a_ref[...].astype(BF16), b_ref[...],
                         preferred_element_type=F32).astype(o_ref.dtype)


def _matmul(a, b, *, tm=512, tn=512, out_dtype=F32):
    M, K = a.shape
    N = b.shape[1]
    tm = min(tm, M)
    tn = min(tn, N)
    assert M % tm == 0 and N % tn == 0, (M, N, tm, tn)
    return pl.pallas_call(
        _mm_kernel,
        grid=(M // tm, N // tn),
        in_specs=[pl.BlockSpec((tm, K), lambda i, j: (i, 0)),
                  pl.BlockSpec((K, tn), lambda i, j: (0, j))],
        out_specs=pl.BlockSpec((tm, tn), lambda i, j: (i, j)),
        out_shape=jax.ShapeDtypeStruct((M, N), out_dtype),
        compiler_params=_params(("parallel", "arbitrary")),
        name="matmul",
    )(a, b.astype(BF16))


def _modulated_norm(x, g, scale, shift):
    ms = jnp.mean(x * x, axis=-1, keepdims=True)
    return (x * lax.rsqrt(ms + EPS) * g) * (1.0 + scale) + shift


def _norm_mm_kernel(x_ref, g_ref, sc_ref, sh_ref, w_ref, o_ref, h_scr):
    @pl.when(pl.program_id(1) == 0)
    def _():
        h = _modulated_norm(x_ref[...], g_ref[...], sc_ref[0], sh_ref[0])
        h_scr[...] = h.astype(BF16)

    o_ref[...] = jnp.dot(h_scr[...], w_ref[...], preferred_element_type=F32)


def _mod_index(nb, rows_per_batch, tm):
    if nb == 1:
        return lambda i, j: (0, 0, 0)
    assert rows_per_batch % tm == 0
    return lambda i, j: (i * tm // rows_per_batch, 0, 0)


def _norm_matmul(x, g, scale, shift, w, rows_per_batch, *, tm=1024, tn=512):
    M, K = x.shape
    N = w.shape[1]
    nb = scale.shape[0]
    tm = min(tm, M, rows_per_batch) if nb > 1 else min(tm, M)
    assert M % tm == 0 and N % tn == 0
    midx = _mod_index(nb, rows_per_batch, tm)
    return pl.pallas_call(
        _norm_mm_kernel,
        grid=(M // tm, N // tn),
        in_specs=[pl.BlockSpec((tm, K), lambda i, j: (i, 0)),
                  pl.BlockSpec((1, K), lambda i, j: (0, 0)),
                  pl.BlockSpec((1, 1, K), midx),
                  pl.BlockSpec((1, 1, K), midx),
                  pl.BlockSpec((K, tn), lambda i, j: (0, j))],
        out_specs=pl.BlockSpec((tm, tn), lambda i, j: (i, j)),
        out_shape=jax.ShapeDtypeStruct((M, N), F32),
        scratch_shapes=[pltpu.VMEM((tm, K), BF16)],
        compiler_params=_params(("parallel", "arbitrary")),
        name="norm_matmul",
    )(x, g.reshape(1, K), scale, shift, w)


def _ffn_kernel(x_ref, g_ref, sc_ref, sh_ref, gt_ref, w1_ref, w2_ref, o_ref, h_scr, acc_scr):
    f = pl.program_id(1)

    @pl.when(f == 0)
    def _():
        h = _modulated_norm(x_ref[...], g_ref[...], sc_ref[0], sh_ref[0])
        h_scr[...] = h.astype(BF16)
        acc_scr[...] = jnp.zeros_like(acc_scr)

    u = jnp.dot(h_scr[...], w1_ref[...], preferred_element_type=F32)
    u = jnp.square(jnp.maximum(u, 0.0))
    acc_scr[...] += jnp.dot(u.astype(BF16), w2_ref[...], preferred_element_type=F32)

    @pl.when(f == pl.num_programs(1) - 1)
    def _():
        o_ref[...] = x_ref[...] + gt_ref[0] * acc_scr[...]


def _ffn(x, g, scale, shift, gate, w1, w2, rows_per_batch, *, tm=512, tf=512):
    M, K = x.shape
    F = w1.shape[1]
    nb = scale.shape[0]
    tm = min(tm, M, rows_per_batch) if nb > 1 else min(tm, M)
    assert M % tm == 0 and F % tf == 0
    midx = _mod_index(nb, rows_per_batch, tm)
    return pl.pallas_call(
        _ffn_kernel,
        grid=(M // tm, F // tf),
        in_specs=[pl.BlockSpec((tm, K), lambda i, j: (i, 0)),
                  pl.BlockSpec((1, K), lambda i, j: (0, 0)),
                  pl.BlockSpec((1, 1, K), midx),
                  pl.BlockSpec((1, 1, K), midx),
                  pl.BlockSpec((1, 1, K), midx),
                  pl.BlockSpec((K, tf), lambda i, j: (0, j)),
                  pl.BlockSpec((tf, K), lambda i, j: (j, 0))],
        out_specs=pl.BlockSpec((tm, K), lambda i, j: (i, 0)),
        out_shape=jax.ShapeDtypeStruct((M, K), F32),
        scratch_shapes=[pltpu.VMEM((tm, K), BF16), pltpu.VMEM((tm, K), F32)],
        compiler_params=_params(("parallel", "arbitrary")),
        name="ffn",
    )(x, g.reshape(1, K), scale, shift, gate, w1, w2)


def _attn_kernel(q_ref, k_ref, v_ref, o_ref, *, scale):
    q = q_ref[0, 0]
    k = k_ref[0, 0]
    s = lax.dot_general(q, k, (((1,), (1,)), ((), ())), preferred_element_type=F32) * scale
    m = jnp.max(s, axis=-1, keepdims=True)
    p = jnp.exp(s - m)
    l = jnp.sum(p, axis=-1, keepdims=True)
    o = jnp.dot(p.astype(BF16), v_ref[0, 0], preferred_element_type=F32)
    o_ref[0] = o / l


def _attention(q, k, v, *, tq=256):
    B, H, T, Dk = q.shape
    S = k.shape[2]
    Dv = v.shape[3]
    tq = min(tq, T)
    return pl.pallas_call(
        functools.partial(_attn_kernel, scale=Dk ** -0.5),
        grid=(B, H, T // tq),
        in_specs=[pl.BlockSpec((1, 1, tq, Dk), lambda b, h, i: (b, h, i, 0)),
                  pl.BlockSpec((1, 1, S, Dk), lambda b, h, i: (b, h, 0, 0)),
                  pl.BlockSpec((1, 1, S, Dv), lambda b, h, i: (b, h, 0, 0))],
        out_specs=pl.BlockSpec((1, tq, Dv), lambda b, h, i: (b, i, h)),
        out_shape=jax.ShapeDtypeStruct((B, T, H * Dv), F32),
        compiler_params=_params(("parallel", "parallel", "arbitrary")),
        name="attention",
    )(q, k, v)


def _bdot(a, b, dims=((1,), (0,))):
    return lax.dot_general(a.astype(BF16), b.astype(BF16), (dims, ((), ())),
                           preferred_element_type=F32)


_NT = ((1,), (1,))
_TN = ((0,), (0,))
PAIR = 2 * HEAD_RWKV


def _scan_kernel(r_ref, v_ref, kk_ref, k_ref, a_ref, ld_ref, s0_ref, y_ref, sf_ref, s_scr,
                 *, chunk, pairs):
    d = pl.program_id(0)
    c = pl.program_id(3)
    C = chunk
    N = HEAD_RWKV

    @pl.when(c == 0)
    def _():
        s_scr[...] = s0_ref[0, 0]

    sgn = 1 - 2 * d
    row = lax.broadcasted_iota(jnp.int32, (C, PAIR), 0)
    lane = lax.broadcasted_iota(jnp.int32, (C, PAIR), 1)
    col = lane & (N - 1)
    rel = (row - col) * sgn
    incl = rel >= 0
    strict = rel > 0
    eye = row == col
    head0 = lane < N
    ti = lax.broadcasted_iota(jnp.int32, (C, C), 0)
    tj = lax.broadcasted_iota(jnp.int32, (C, C), 1)
    tri = jnp.where((ti - tj) * sgn >= 0, 1.0, 0.0).astype(BF16)

    def bd(x):
        return jnp.concatenate([jnp.where(head0, x, 0.0), jnp.where(head0, 0.0, x)],
                               axis=0).astype(BF16)

    def diag_blocks(x):
        return jnp.where(head0, x[:N], x[N:])

    ld = ld_ref[0, 0]
    ld_hi = ld.astype(BF16)
    rem = ld - ld_hi.astype(F32)
    ld_mid = rem.astype(BF16)
    ld_lo = (rem - ld_mid.astype(F32)).astype(BF16)
    cum = (jnp.dot(tri, ld_hi, preferred_element_type=F32)
           + jnp.dot(tri, ld_mid, preferred_element_type=F32)
           + jnp.dot(tri, ld_lo, preferred_element_type=F32))
    tot = jnp.sum(ld, axis=0, keepdims=True)
    e_neg = jnp.exp(-cum)
    e_end = jnp.exp(tot - cum)
    kk = kk_ref[0]
    b = kk * a_ref[0, 0]
    k = k_ref[0, 0]
    a_t = -kk * jnp.exp(cum - ld)
    r_t = r_ref[0] * jnp.exp(cum)
    b_t = b * e_neg
    k_t = k * e_neg
    b_d = b * e_end
    k_d = k * e_end
    v = v_ref[0]
    e_tot = jnp.exp(tot)

    P = range(pairs)
    sls = [slice(p * PAIR, (p + 1) * PAIR) for p in P]
    At = [a_t[:, s] for s in sls]
    Rt = [r_t[:, s] for s in sls]
    V = [v[:, s] for s in sls]
    bdv = [bd(V[p]) for p in P]
    sc = [_bdot(jnp.concatenate([At[p], Rt[p]], axis=0),
                jnp.concatenate([bd(b_t[:, sls[p]]), bd(k_t[:, sls[p]])], axis=0), _NT)
          for p in P]
    L = [jnp.where(strict, sc[p][:C, :PAIR], 0.0) for p in P]
    a_ak = [jnp.where(strict, sc[p][:C, PAIR:], 0.0) for p in P]
    p_rb = [jnp.where(incl, sc[p][C:, :PAIR], 0.0) for p in P]
    p_rk = [jnp.where(incl, sc[p][C:, PAIR:], 0.0) for p in P]
    akv = [_bdot(a_ak[p], bdv[p]) for p in P]
    x = [jnp.where(eye, 1.0, 0.0) + L[p] for p in P]
    lp = [_bdot(L[p], bd(L[p])) for p in P]
    n = 2
    while n < C:
        if 2 * n < C:
            o = [_bdot(lp[p], jnp.concatenate([bd(x[p]), bd(lp[p])], axis=1)) for p in P]
            x = [x[p] + o[p][:, :PAIR] for p in P]
            lp = [o[p][:, PAIR:] for p in P]
        else:
            x = [x[p] + _bdot(lp[p], bd(x[p])) for p in P]
        n *= 2
    z = [_bdot(x[p], jnp.concatenate([bd(At[p]), bd(akv[p])], axis=1)) for p in P]
    a_p = [z[p][:, :PAIR] for p in P]
    u_loc = [z[p][:, PAIR:] for p in P]
    f = [_bdot(p_rb[p], jnp.concatenate([bd(a_p[p]), bd(u_loc[p])], axis=1)) for p in P]
    y_rk = [_bdot(p_rk[p], bdv[p]) for p in P]
    g = [diag_blocks(_bdot(a_p[p], b_d[:, sls[p]], _TN)) for p in P]
    s_loc = [diag_blocks(_bdot(jnp.concatenate([u_loc[p], V[p]], axis=0),
                               jnp.concatenate([b_d[:, sls[p]], k_d[:, sls[p]]], axis=0), _TN))
             for p in P]
    S = [s_scr[p] for p in P]
    ys = [_bdot(Rt[p] + f[p][:, :PAIR], bd(S[p]), _NT) + (f[p][:, PAIR:] + y_rk[p]) for p in P]
    for p in P:
        s_scr[p] = S[p] * e_tot[:, sls[p]] + _bdot(S[p], bd(g[p])) + s_loc[p]
    y_ref[0, 0] = jnp.concatenate(ys, axis=-1)

    @pl.when(c == pl.num_programs(3) - 1)
    def _():
        sf_ref[0, 0] = s_scr[...]


def _rwkv_scan(r, v, kk, k, a, ld, s0):
    B, T, Rd = r.shape
    C = SCAN_CHUNK
    pb = SCAN_PAIRS
    nC = T // C
    W = pb * PAIR
    n_pairs = H_RWKV // 2
    N = HEAD_RWKV

    def to_pairs(s):
        return s.reshape(B, N_DIR, n_pairs, 2, N, N).swapaxes(3, 4).reshape(B, N_DIR, n_pairs, N, PAIR)

    def from_pairs(s):
        return s.reshape(B, N_DIR, n_pairs, N, 2, N).swapaxes(3, 4).reshape(B, N_DIR, H_RWKV, N, N)

    def tmap(dd, cc):
        return cc + dd * (nC - 1 - 2 * cc)

    shared = pl.BlockSpec((1, C, W), lambda dd, bb, g, cc: (bb, tmap(dd, cc), g))
    per_dir = pl.BlockSpec((1, 1, C, W), lambda dd, bb, g, cc: (dd, bb, tmap(dd, cc), g))
    state = pl.BlockSpec((1, 1, pb, N, PAIR), lambda dd, bb, g, cc: (bb, dd, g, 0, 0))
    y, s_fin = pl.pallas_call(
        functools.partial(_scan_kernel, chunk=C, pairs=pb),
        grid=(N_DIR, B, n_pairs // pb, nC),
        in_specs=[shared, shared, shared, per_dir, per_dir, per_dir, state],
        out_specs=[per_dir, state],
        out_shape=[jax.ShapeDtypeStruct((N_DIR, B, T, Rd), F32),
                   jax.ShapeDtypeStruct((B, N_DIR, n_pairs, N, PAIR), F32)],
        scratch_shapes=[pltpu.VMEM((pb, N, PAIR), F32)],
        compiler_params=_params(("parallel", "parallel", "parallel", "arbitrary")),
        name="rwkv_scan",
    )(r, v, kk, k, a, ld, to_pairs(s0))
    return y, from_pairs(s_fin)


def _rms_norm(x, g):
    return x * lax.rsqrt(jnp.mean(x * x, axis=-1, keepdims=True) + EPS) * g


def _rotate_pairs(x, ang):
    cos = jnp.cos(ang)[None, :, None, :]
    sin = jnp.sin(ang)[None, :, None, :]
    x1, x2 = x[..., :AXIS_PAIRS], x[..., AXIS_PAIRS:]
    return jnp.concatenate([x1 * cos - x2 * sin, x2 * cos + x1 * sin], axis=-1)


def _axial_rope(x):
    T = x.shape[1]
    rows = T // GRID_W
    row = jnp.repeat(jnp.arange(rows, dtype=F32), GRID_W)
    col = jnp.tile(jnp.arange(GRID_W, dtype=F32), rows)
    inv_freq = jnp.power(ROPE_THETA, -jnp.arange(AXIS_PAIRS, dtype=F32) / AXIS_PAIRS)
    x_nope = x[..., :NOPE_DIM]
    x_row = x[..., NOPE_DIM:NOPE_DIM + AXIS_DIM]
    x_col = x[..., NOPE_DIM + AXIS_DIM:]
    return jnp.concatenate([x_nope, _rotate_pairs(x_row, row[:, None] * inv_freq),
                            _rotate_pairs(x_col, col[:, None] * inv_freq)], axis=-1)


def _keys_values(ckv, kr, p):
    B, L, _ = ckv.shape
    kv = _matmul(_rms_norm(ckv, p['kv_norm']).reshape(B * L, KV_RANK), p['w_kv_up'])
    kv = kv.reshape(B, L, H_MLA, NOPE_DIM + V_HEAD)
    k_rope = jnp.broadcast_to(kr[:, :, None, :], (B, L, H_MLA, ROPE_DIM))
    k = _rms_norm(jnp.concatenate([kv[..., :NOPE_DIM], k_rope], axis=-1), p['k_norm'])
    return k, kv[..., NOPE_DIM:]


def _centred_conv(x, w):
    xp = jnp.pad(x, ((0, 0), (1, 1), (0, 0)))
    return xp[:, :-2] * w[0] + xp[:, 1:-1] * w[1] + xp[:, 2:] * w[2]


def _rwkv_branch(z_rkv, z_wd, z_ad, z_gd, p, s0):
    B, T, _ = z_rkv.shape
    M = B * T
    rkv = _centred_conv(z_rkv, p['conv_rkv'])
    r, k, v = jnp.split(rkv, 3, axis=-1)
    kh = (k * p['k_k']).reshape(B, T, H_RWKV, HEAD_RWKV)
    kk = (kh * lax.rsqrt(jnp.sum(kh * kh, axis=-1, keepdims=True) + 1e-12)).reshape(B, T, R_DIM)
    g = _matmul(jax.nn.sigmoid(z_gd).reshape(M, G_LORA), p['g_up']).reshape(B, T, R_DIM)
    ks, As, lds = [], [], []
    for d in range(N_DIR):
        wd = jnp.tanh(z_wd[..., d * W_LORA:(d + 1) * W_LORA]).reshape(M, W_LORA)
        ad = z_ad[..., d * A_LORA:(d + 1) * A_LORA].reshape(M, A_LORA)
        w_log = -jax.nn.softplus(-(p['w0'][d] + _matmul(wd, p['w_up'][d]))) - 0.5
        lds.append((-jnp.exp(w_log)).reshape(B, T, R_DIM))
        a = jax.nn.sigmoid(p['a0'][d] + _matmul(ad, p['a_up'][d])).reshape(B, T, R_DIM)
        As.append(a)
        ks.append(k * (1.0 + (a - 1.0) * p['k_a']))
    k_d = jnp.stack(ks)
    y, s_fin = _rwkv_scan(r, v, kk, k_d, jnp.stack(As), jnp.stack(lds), s0)
    ysum = (y[0] + y[1]).reshape(B, T, H_RWKV, HEAD_RWKV)
    mu = jnp.mean(ysum, axis=-1, keepdims=True)
    var = jnp.mean(jnp.square(ysum - mu), axis=-1, keepdims=True)
    yn = ((ysum - mu) * lax.rsqrt(var + LNX_EPS)).reshape(B, T, R_DIM)
    o = yn * p['lnx_w'] + p['lnx_b']
    rk = (r[None] * k_d).reshape(N_DIR, B, T, H_RWKV, HEAD_RWKV) * p['r_k']
    bonus = jnp.sum(jnp.sum(rk, axis=-1, keepdims=True), axis=0) * v.reshape(B, T, H_RWKV, HEAD_RWKV)
    o = o + bonus.reshape(B, T, R_DIM)
    return o * g, s_fin


def _trunk_layer(x, mod, p, cache):
    B, T, _ = x.shape
    M = B * T
    nb = mod.shape[0]
    shift1, scale1, gate1, shift2, scale2, gate2 = [
        m.reshape(nb, 1, D_MODEL) for m in jnp.split(mod, 6, axis=-1)]
    xf = x.reshape(M, D_MODEL)
    z = _norm_matmul(xf, p['norm1'], scale1, shift1, p['w_in'], T).reshape(B, T, -1)
    z_q, z_ckv, z_kr, z_rkv, z_wd, z_ad, z_gd, z_gate = [
        z[..., o:o + s] for o, s in zip(p['in_offsets'], IN_SIZES)]
    q = _rms_norm(z_q.reshape(B, T, H_MLA, QK_HEAD), p['q_norm'])
    k, v = _keys_values(z_ckv, z_kr, p)
    if cache is None:
        s0 = jnp.zeros((B, N_DIR, H_RWKV, HEAD_RWKV, HEAD_RWKV), F32)
    else:
        ckv_ctx, kr_ctx, s0 = cache
        q = _axial_rope(q)
        k = _axial_rope(k)
        k_ctx, v_ctx = _keys_values(ckv_ctx, kr_ctx, p)
        k = jnp.concatenate([k, k_ctx], axis=1)
        v = jnp.concatenate([v, v_ctx], axis=1)
    o_mla = _attention(jnp.swapaxes(q, 1, 2).astype(BF16), jnp.swapaxes(k, 1, 2).astype(BF16),
                       jnp.swapaxes(v, 1, 2).astype(BF16))
    o_rwkv, s_final = _rwkv_branch(z_rkv, z_wd, z_ad, z_gd, p, s0)
    gates = jax.nn.sigmoid(z_gate).reshape(M, 2 * D_MODEL)
    merged = (gates[:, :D_MODEL] * _matmul(o_mla.reshape(M, MLA_WIDTH), p['w_br_mla'])
              + gates[:, D_MODEL:] * _matmul(o_rwkv.reshape(M, R_DIM), p['w_br_rwkv']))
    x1 = x + gate1 * _matmul(merged, p['w_out']).reshape(B, T, D_MODEL)
    x2 = _ffn(x1.reshape(M, D_MODEL), p['norm2'], scale2, shift2, gate2,
              p['w_ff_in'], p['w_ff_out'], T).reshape(B, T, D_MODEL)
    return x2, (z_ckv, z_kr, s_final)


def _pad_cols(w, total):
    parts = []
    off = 0
    for s in IN_SIZES:
        blk = w[:, off:off + s]
        pad = (-s) % 128
        if pad:
            blk = jnp.pad(blk, ((0, 0), (0, pad)))
        parts.append(blk)
        off += s
    out = jnp.concatenate(parts, axis=1)
    pad = total - out.shape[1]
    return jnp.pad(out, ((0, 0), (0, pad))) if pad else out


def kernel(x_prompt, x_sample, cache_mla_ckv, cache_mla_kr, state_rwkv, c, c_ctx,
           norm1, w_ada, b_ada, w_in, q_norm, kv_norm, w_kv_up, k_norm, conv_rkv,
           k_k, k_a, r_k, w0, w_up, a0, a_up, g_up, lnx_w, lnx_b,
           w_br_mla, w_br_rwkv, w_out, norm2, w_ff_in, w_ff_out):
    x_p, x_s = x_prompt, x_sample
    n_lat = c.shape[0]
    padded_sizes = [s + (-s) % 128 for s in IN_SIZES]
    in_offsets = [sum(padded_sizes[:i]) for i in range(len(IN_SIZES))]
    in_total = sum(padded_sizes)
    in_total += (-in_total) % 512
    ckv_list, kr_list, st_list = [], [], []
    for l in range(DEPTH):
        p = {
            'norm1': norm1[l], 'q_norm': q_norm[l], 'kv_norm': kv_norm[l], 'k_norm': k_norm[l],
            'conv_rkv': conv_rkv[l], 'k_k': k_k[l], 'k_a': k_a[l], 'r_k': r_k[l],
            'w0': w0[l], 'w_up': w_up[l], 'a0': a0[l], 'a_up': a_up[l], 'g_up': g_up[l],
            'lnx_w': lnx_w[l], 'lnx_b': lnx_b[l], 'norm2': norm2[l],
            'w_in': _pad_cols(w_in[l].astype(BF16), in_total), 'in_offsets': in_offsets,
            'w_kv_up': w_kv_up[l].astype(BF16),
            'w_br_mla': w_br_mla[l].astype(BF16), 'w_br_rwkv': w_br_rwkv[l].astype(BF16),
            'w_out': w_out[l].astype(BF16),
            'w_ff_in': w_ff_in[l].astype(BF16), 'w_ff_out': w_ff_out[l].astype(BF16),
        }
        cond = jnp.concatenate([c, c_ctx[None]], axis=0)
        cond = jnp.pad(jax.nn.silu(cond), ((0, (-cond.shape[0]) % 16), (0, 0)))
        mod = _matmul(cond, w_ada[l], tm=16, tn=1024)[:n_lat + 1] + b_ada[l]
        x_p, (ckv_l, kr_l, st_l) = _trunk_layer(x_p, mod[n_lat:], p, None)
        ckv_list.append(ckv_l)
        kr_list.append(kr_l)
        st_list.append(st_l)
        x_s, _ = _trunk_layer(x_s, mod[:n_lat], p,
                              (cache_mla_ckv[:, l], cache_mla_kr[:, l], state_rwkv[:, l]))
    return (x_p, x_s, jnp.stack(ckv_list, axis=1), jnp.stack(kr_list, axis=1),
            jnp.stack(st_list, axis=1))
```

```python
import functools

import jax
import jax.numpy as jnp
from jax import lax
from jax.experimental import pallas as pl
from jax.experimental.pallas import tpu as pltpu

D_MODEL = 2048
DEPTH = 1
GRID_W = 64
H_MLA = 16
NOPE_DIM = 128
ROPE_DIM = 64
QK_HEAD = NOPE_DIM + ROPE_DIM
V_HEAD = 128
KV_RANK = 512
ROPE_THETA = 10000.0
AXIS_DIM = ROPE_DIM // 2
AXIS_PAIRS = AXIS_DIM // 2
MLA_WIDTH = H_MLA * V_HEAD
Q_DIM = H_MLA * QK_HEAD
HEAD_RWKV = 64
H_RWKV = D_MODEL // HEAD_RWKV
R_DIM = H_RWKV * HEAD_RWKV
W_LORA = 64
A_LORA = 64
G_LORA = 128
N_DIR = 2
LNX_EPS = 64e-5
D_FF = 4 * D_MODEL
EPS = 1e-6
IN_SIZES = (Q_DIM, KV_RANK, ROPE_DIM, 3 * R_DIM, N_DIR * W_LORA, N_DIR * A_LORA, G_LORA, 2 * D_MODEL)
IN_DIM = sum(IN_SIZES)

F32 = jnp.float32
BF16 = jnp.bfloat16
VMEM_LIMIT = 56 * 1024 * 1024
SCAN_CHUNK = 64
SCAN_PAIRS = 16


def _params(sem):
    return pltpu.CompilerParams(dimension_semantics=sem, vmem_limit_bytes=VMEM_LIMIT)


def _mm_kernel(a_ref, b_ref, o_ref):
    o_ref[...] = jnp.dot(a_ref[...].astype(BF16), b_ref[...],
                         preferred_element_type=F32).astype(o_ref.dtype)


def _matmul(a, b, *, tm=512, tn=512, out_dtype=F32):
    M, K = a.shape
    N = b.shape[1]
    tm = min(tm, M)
    tn = min(tn, N)
    assert M % tm == 0 and N % tn == 0, (M, N, tm, tn)
    return pl.pallas_call(
        _mm_kernel,
        grid=(M // tm, N // tn),
        in_specs=[pl.BlockSpec((tm, K), lambda i, j: (i, 0)),
                  pl.BlockSpec((K, tn), lambda i, j: (0, j))],
        out_specs=pl.BlockSpec((tm, tn), lambda i, j: (i, j)),
        out_shape=jax.ShapeDtypeStruct((M, N), out_dtype),
        compiler_params=_params(("parallel", "arbitrary")),
        name="matmul",
    )(a, b.astype(BF16))


def _modulated_norm(x, g, scale, shift):
    ms = jnp.mean(x * x, axis=-1, keepdims=True)
    return (x * lax.rsqrt(ms + EPS) * g) * (1.0 + scale) + shift


def _mod_index(nb, rows_per_batch, tm):
    if nb == 1:
        return lambda i, j: (0, 0, 0)
    assert rows_per_batch % tm == 0
    return lambda i, j: (i * tm // rows_per_batch, 0, 0)


def _row_tile(tm, M, nb, rows_per_batch):
    tm = min(tm, M, rows_per_batch) if nb > 1 else min(tm, M)
    assert M % tm == 0
    return tm


def _sigmoid(x):
    return 1.0 / (1.0 + jnp.exp(-x))


def _swap16(x):
    lane = lax.broadcasted_iota(jnp.int32, x.shape, 1)
    return jnp.where((lane & AXIS_PAIRS) == 0,
                     pltpu.roll(x, 128 - AXIS_PAIRS, 1), pltpu.roll(x, AXIS_PAIRS, 1))


def _epi_plain(acc, extra, outs):
    outs[0][...] = acc.astype(outs[0].dtype)


def _epi_sigmoid(acc, extra, outs):
    outs[0][...] = _sigmoid(acc).astype(outs[0].dtype)


SMALL_LORA = KV_RANK + 128
SMALL_COLS = SMALL_LORA + 3 * 128


def _epi_small(acc, extra, outs):
    ckv_ref, kr_ref, lora_ref = outs
    ckv_ref[...] = acc[:, :KV_RANK]
    kr_ref[...] = acc[:, KV_RANK:KV_RANK + ROPE_DIM]
    o = SMALL_LORA
    wa = acc[:, o:o + 256]
    lane = lax.broadcasted_iota(jnp.int32, wa.shape, 1)
    wa = jnp.where((lane & W_LORA) == 0, jnp.tanh(wa), wa)
    lora_ref[...] = jnp.concatenate([wa, _sigmoid(acc[:, o + 256:o + 384])], axis=1)


Q_PAIR = 2 * QK_HEAD


def _epi_q(acc, extra, outs, *, rope):
    g_ref = extra[0]
    tm, tn = acc.shape
    lo = lax.broadcasted_iota(jnp.int32, (tm, 128), 1) < ROPE_DIM
    parts = []
    for p in range(tn // Q_PAIR):
        z = acc[:, p * Q_PAIR:(p + 1) * Q_PAIR]
        g = g_ref[:, p * Q_PAIR:(p + 1) * Q_PAIR]
        n0, n1, rp = z[:, :128], z[:, 128:256], z[:, 256:]
        rp2 = rp * rp
        s0 = (jnp.sum(n0 * n0, axis=-1, keepdims=True)
              + jnp.sum(jnp.where(lo, rp2, 0.0), axis=-1, keepdims=True))
        s1 = (jnp.sum(n1 * n1, axis=-1, keepdims=True)
              + jnp.sum(jnp.where(lo, 0.0, rp2), axis=-1, keepdims=True))
        r0 = lax.rsqrt(s0 * (1.0 / QK_HEAD) + EPS)
        r1 = lax.rsqrt(s1 * (1.0 / QK_HEAD) + EPS)
        qr = rp * jnp.where(lo, r0, r1) * g[:, 256:]
        if rope:
            qr = qr * extra[1][...] + _swap16(qr) * extra[2][...]
        parts += [n0 * r0 * g[:, :128], n1 * r1 * g[:, 128:256], qr]
    outs[0][...] = jnp.concatenate(parts, axis=1).astype(outs[0].dtype)


def _norm_mm_kernel(*refs, n_extra, n_out, epilogue):
    x_ref, g_ref, sc_ref, sh_ref, w_ref = refs[:5]
    extra = refs[5:5 + n_extra]
    outs = refs[5 + n_extra:5 + n_extra + n_out]
    h_scr = refs[-1]

    @pl.when(pl.program_id(1) == 0)
    def _():
        h = _modulated_norm(x_ref[...], g_ref[...], sc_ref[0], sh_ref[0])
        h_scr[...] = h.astype(BF16)

    epilogue(jnp.dot(h_scr[...], w_ref[...], preferred_element_type=F32), extra, outs)


def _norm_matmul(x, g, scale, shift, w, rows_per_batch, *, tm, tn, epilogue, outs,
                 extra=(), extra_specs=(), name):
    M, K = x.shape
    N = w.shape[1]
    nb = scale.shape[0]
    tm = _row_tile(tm, M, nb, rows_per_batch)
    assert N % tn == 0
    midx = _mod_index(nb, rows_per_batch, tm)
    res = pl.pallas_call(
        functools.partial(_norm_mm_kernel, n_extra=len(extra), n_out=len(outs), epilogue=epilogue),
        grid=(M // tm, N // tn),
        in_specs=[pl.BlockSpec((tm, K), lambda i, j: (i, 0)),
                  pl.BlockSpec((1, K), lambda i, j: (0, 0)),
                  pl.BlockSpec((1, 1, K), midx),
                  pl.BlockSpec((1, 1, K), midx),
                  pl.BlockSpec((K, tn), lambda i, j: (0, j))] + [s(tm) for s in extra_specs],
        out_specs=[pl.BlockSpec((tm, c), lambda i, j: (i, j)) for c, _, _ in outs],
        out_shape=[jax.ShapeDtypeStruct((M, n), dt) for _, n, dt in outs],
        scratch_shapes=[pltpu.VMEM((tm, K), BF16)],
        compiler_params=_params(("parallel", "arbitrary")),
        name=name,
    )(x, g.reshape(1, K), scale, shift, w, *extra)
    return res


K_WIDTH = 2 * NOPE_DIM


def _kv_kernel(*refs, rope):
    if rope:
        (ckv_ref, kr_ref, gkv_ref, gk_ref, cos_ref, sin_ref, w_ref, k_ref, v_ref,
         a_scr, kr_scr, ss_scr) = refs
    else:
        ckv_ref, kr_ref, gkv_ref, gk_ref, w_ref, k_ref, v_ref, a_scr, kr_scr, ss_scr = refs
    h = pl.program_id(1)

    @pl.when(h == 0)
    def _():
        c = ckv_ref[...]
        a = c * lax.rsqrt(jnp.mean(c * c, axis=-1, keepdims=True) + EPS) * gkv_ref[...]
        a_scr[...] = a.astype(BF16)
        kr = kr_ref[...]
        ss_scr[...] = jnp.broadcast_to(0.5 * jnp.sum(kr * kr, axis=-1, keepdims=True), ss_scr.shape)
        krg = kr * gk_ref[:, NOPE_DIM:]
        if rope:
            krg = krg * cos_ref[...] + _swap16(krg) * sin_ref[...]
        kr_scr[...] = krg

    acc = jnp.dot(a_scr[...], w_ref[...], preferred_element_type=F32)
    kn = acc[:, :NOPE_DIM]
    ssq = jnp.sum(kn * kn, axis=-1, keepdims=True) + ss_scr[:, :1]
    rstd = lax.rsqrt(ssq * (1.0 / QK_HEAD) + EPS)
    lane = lax.broadcasted_iota(jnp.int32, kn.shape, 1)
    mine = (lane >> 6) == (h & 1)
    krh = jnp.where(mine, kr_scr[...] * rstd, 0.0)
    k_ref[0] = jnp.concatenate([kn * rstd * gk_ref[:, :NOPE_DIM], krh], axis=1).astype(BF16)
    v_ref[0] = acc[:, NOPE_DIM:].astype(BF16)


def _keys_values(ckv, kr, p, tables, *, tm):
    Mk = ckv.shape[0]
    tm = min(tm, Mk)
    assert Mk % tm == 0
    rope = tables is not None
    kr2 = jnp.concatenate([kr, kr], axis=-1)
    gk = jnp.concatenate([p['k_norm'], p['k_norm'][NOPE_DIM:]]).reshape(1, K_WIDTH)
    row = lambda i, h: (i, 0)
    fixed = lambda i, h: (0, 0)
    in_specs = [pl.BlockSpec((tm, KV_RANK), row), pl.BlockSpec((tm, 128), row),
                pl.BlockSpec((1, KV_RANK), fixed), pl.BlockSpec((1, K_WIDTH), fixed)]
    args = [ckv, kr2, p['kv_norm'].reshape(1, KV_RANK), gk]
    if rope:
        in_specs += [pl.BlockSpec((tm, 128), fixed), pl.BlockSpec((tm, 128), fixed)]
        args += list(tables)
    in_specs.append(pl.BlockSpec((KV_RANK, NOPE_DIM + V_HEAD), lambda i, h: (0, h)))
    args.append(p['w_kv_up'])
    return pl.pallas_call(
        functools.partial(_kv_kernel, rope=rope),
        grid=(Mk // tm, H_MLA),
        in_specs=in_specs,
        out_specs=[pl.BlockSpec((1, tm, K_WIDTH), lambda i, h: (h, i, 0)),
                   pl.BlockSpec((1, tm, V_HEAD), lambda i, h: (h, i, 0))],
        out_shape=[jax.ShapeDtypeStruct((H_MLA, Mk, K_WIDTH), BF16),
                   jax.ShapeDtypeStruct((H_MLA, Mk, V_HEAD), BF16)],
        scratch_shapes=[pltpu.VMEM((tm, KV_RANK), BF16), pltpu.VMEM((tm, 128), F32),
                        pltpu.VMEM((tm, 128), F32)],
        compiler_params=_params(("parallel", "arbitrary")),
        name="keys_values",
    )(*args)


def _attn_kernel(q_ref, k_ref, v_ref, o_ref):
    q = q_ref[0]
    qr = q[:, 2 * NOPE_DIM:]
    outs = []
    for h in range(2):
        qh = jnp.concatenate([q[:, h * NOPE_DIM:(h + 1) * NOPE_DIM], qr], axis=1)
        s = lax.dot_general(qh, k_ref[h], (((1,), (1,)), ((), ())), preferred_element_type=F32)
        m = jnp.max(s, axis=-1, keepdims=True)
        p = jnp.exp(s - m)
        l = jnp.sum(p, axis=-1, keepdims=True)
        outs.append(jnp.dot(p.astype(BF16), v_ref[h], preferred_element_type=F32) / l)
    o_ref[0] = jnp.concatenate(outs, axis=1).astype(o_ref.dtype)


def _attention(q, k, v, S, *, tq=256):
    B, T, _ = q.shape
    tq = min(tq, T)
    return pl.pallas_call(
        _attn_kernel,
        grid=(B, H_MLA // 2, T // tq),
        in_specs=[pl.BlockSpec((1, tq, Q_PAIR), lambda b, p, i: (b, i, p)),
                  pl.BlockSpec((2, S, K_WIDTH), lambda b, p, i: (p, b, 0)),
                  pl.BlockSpec((2, S, V_HEAD), lambda b, p, i: (p, b, 0))],
        out_specs=pl.BlockSpec((1, tq, 2 * V_HEAD), lambda b, p, i: (b, i, p)),
        out_shape=jax.ShapeDtypeStruct((B, T, MLA_WIDTH), BF16),
        compiler_params=_params(("parallel", "parallel", "arbitrary")),
        name="attention",
    )(q, k, v)


def _branch_kernel(a_ref, b_ref, ga_ref, gb_ref, wa_ref, wb_ref, o_ref):
    ya = jnp.dot(a_ref[...], wa_ref[...], preferred_element_type=F32)
    yb = jnp.dot(b_ref[...], wb_ref[...], preferred_element_type=F32)
    o_ref[...] = (ga_ref[...].astype(F32) * ya + gb_ref[...].astype(F32) * yb).astype(o_ref.dtype)


def _branch_merge(a, b, gates, wa, wb, *, tm=1024, tn=512):
    M, K = a.shape
    N = wa.shape[1]
    tm = min(tm, M)
    nj = N // tn
    return pl.pallas_call(
        _branch_kernel,
        grid=(M // tm, nj),
        in_specs=[pl.BlockSpec((tm, K), lambda i, j: (i, 0)),
                  pl.BlockSpec((tm, K), lambda i, j: (i, 0)),
                  pl.BlockSpec((tm, tn), lambda i, j: (i, j)),
                  pl.BlockSpec((tm, tn), lambda i, j: (i, j + nj)),
                  pl.BlockSpec((K, tn), lambda i, j: (0, j)),
                  pl.BlockSpec((K, tn), lambda i, j: (0, j))],
        out_specs=pl.BlockSpec((tm, tn), lambda i, j: (i, j)),
        out_shape=jax.ShapeDtypeStruct((M, N), BF16),
        compiler_params=_params(("parallel", "arbitrary")),
        name="branch_merge",
    )(a, b, gates, gates, wa, wb)


def _resid_kernel(m_ref, w_ref, x_ref, gt_ref, o_ref):
    o_ref[...] = x_ref[...] + gt_ref[0] * jnp.dot(m_ref[...], w_ref[...],
                                                  preferred_element_type=F32)


def _out_proj(m, w, x, gate, rows_per_batch, *, tm=1024, tn=512):
    M, K = m.shape
    N = w.shape[1]
    nb = gate.shape[0]
    tm = _row_tile(tm, M, nb, rows_per_batch)
    if nb == 1:
        gidx = lambda i, j: (0, 0, j)
    else:
        gidx = lambda i, j: (i * tm // rows_per_batch, 0, j)
    return pl.pallas_call(
        _resid_kernel,
        grid=(M // tm, N // tn),
        in_specs=[pl.BlockSpec((tm, K), lambda i, j: (i, 0)),
                  pl.BlockSpec((K, tn), lambda i, j: (0, j)),
                  pl.BlockSpec((tm, tn), lambda i, j: (i, j)),
                  pl.BlockSpec((1, 1, tn), gidx)],
        out_specs=pl.BlockSpec((tm, tn), lambda i, j: (i, j)),
        out_shape=jax.ShapeDtypeStruct((M, N), F32),
        compiler_params=_params(("parallel", "arbitrary")),
        name="out_proj",
    )(m, w, x, gate)


def _ffn_kernel(x_ref, g_ref, sc_ref, sh_ref, gt_ref, w1_ref, w2_ref, o_ref, h_scr, acc_scr):
    f = pl.program_id(1)

    @pl.when(f == 0)
    def _():
        h = _modulated_norm(x_ref[...], g_ref[...], sc_ref[0], sh_ref[0])
        h_scr[...] = h.astype(BF16)
        acc_scr[...] = jnp.zeros_like(acc_scr)

    u = jnp.dot(h_scr[...], w1_ref[...], preferred_element_type=F32)
    u = jnp.square(jnp.maximum(u, 0.0))
    acc_scr[...] += jnp.dot(u.astype(BF16), w2_ref[...], preferred_element_type=F32)

    @pl.when(f == pl.num_programs(1) - 1)
    def _():
        o_ref[...] = x_ref[...] + gt_ref[0] * acc_scr[...]


def _ffn(x, g, scale, shift, gate, w1, w2, rows_per_batch, *, tm=512, tf=512):
    M, K = x.shape
    F = w1.shape[1]
    nb = scale.shape[0]
    tm = _row_tile(tm, M, nb, rows_per_batch)
    assert F % tf == 0
    midx = _mod_index(nb, rows_per_batch, tm)
    return pl.pallas_call(
        _ffn_kernel,
        grid=(M // tm, F // tf),
        in_specs=[pl.BlockSpec((tm, K), lambda i, j: (i, 0)),
                  pl.BlockSpec((1, K), lambda i, j: (0, 0)),
                  pl.BlockSpec((1, 1, K), midx),
                  pl.BlockSpec((1, 1, K), midx),
                  pl.BlockSpec((1, 1, K), midx),
                  pl.BlockSpec((K, tf), lambda i, j: (0, j)),
                  pl.BlockSpec((tf, K), lambda i, j: (j, 0))],
        out_specs=pl.BlockSpec((tm, K), lambda i, j: (i, 0)),
        out_shape=jax.ShapeDtypeStruct((M, K), F32),
        scratch_shapes=[pltpu.VMEM((tm, K), BF16), pltpu.VMEM((tm, K), F32)],
        compiler_params=_params(("parallel", "arbitrary")),
        name="ffn",
    )(x, g.reshape(1, K), scale, shift, gate, w1, w2)


def _bdot(a, b, dims=((1,), (0,))):
    return lax.dot_general(a.astype(BF16), b.astype(BF16), (dims, ((), ())),
                           preferred_element_type=F32)


_NT = ((1,), (1,))
_TN = ((0,), (0,))
PAIR = 2 * HEAD_RWKV


def _scan_kernel(z_ref, zp_ref, zn_ref, lo_ref, cw_ref, kkg_ref, ka_ref, rk_ref, w0_ref, a0_ref,
                 wup_ref, aup_ref, s0_ref, y_ref, bon_ref, sf_ref, s_scr, *, chunk, pairs, n_chunks):
    d = pl.program_id(0)
    c = pl.program_id(2)
    C = chunk
    N = HEAD_RWKV

    @pl.when(c == 0)
    def _():
        s_scr[...] = s0_ref[0, 0]

    sgn = 1 - 2 * d
    tt = c + d * (n_chunks - 1 - 2 * c)
    row = lax.broadcasted_iota(jnp.int32, (C, PAIR), 0)
    lane = lax.broadcasted_iota(jnp.int32, (C, PAIR), 1)
    col = lane & (N - 1)
    rel = (row - col) * sgn
    incl = rel >= 0
    strict = rel > 0
    eye = row == col
    head0 = lane < N
    ti = lax.broadcasted_iota(jnp.int32, (C, C), 0)
    tj = lax.broadcasted_iota(jnp.int32, (C, C), 1)
    tri = jnp.where((ti - tj) * sgn >= 0, 1.0, 0.0).astype(BF16)

    def bd(x):
        return jnp.concatenate([jnp.where(head0, x, 0.0), jnp.where(head0, 0.0, x)],
                               axis=0).astype(BF16)

    def diag_blocks(x):
        return jnp.where(head0, x[:N], x[N:])

    def head_sum(x):
        s0 = jnp.sum(jnp.where(head0, x, 0.0), axis=-1, keepdims=True)
        s1 = jnp.sum(jnp.where(head0, 0.0, x), axis=-1, keepdims=True)
        return jnp.where(head0, s0, s1)

    lora = lo_ref[0].astype(BF16)
    u = w0_ref[0] + jnp.dot(lora, wup_ref[0], preferred_element_type=F32)
    softplus = jnp.maximum(-u, 0.0) + jnp.log(1.0 + jnp.exp(-jnp.abs(u)))
    ld = -jnp.exp(-softplus - 0.5)
    a_all = _sigmoid(a0_ref[0] + jnp.dot(lora, aup_ref[0], preferred_element_type=F32))

    ld_hi = ld.astype(BF16)
    rem = ld - ld_hi.astype(F32)
    ld_mid = rem.astype(BF16)
    ld_lo = (rem - ld_mid.astype(F32)).astype(BF16)
    cum = (jnp.dot(tri, ld_hi, preferred_element_type=F32)
           + jnp.dot(tri, ld_mid, preferred_element_type=F32)
           + jnp.dot(tri, ld_lo, preferred_element_type=F32))
    tot = jnp.sum(ld, axis=0, keepdims=True)
    e_cum = jnp.exp(cum)
    e_prev = jnp.exp(cum - ld)
    e_neg = jnp.exp(-cum)
    e_end = jnp.exp(tot - cum)
    e_tot = jnp.exp(tot)

    first = tt == 0
    last = tt == n_chunks - 1

    def conv(off, p):
        sl = slice(off + p * PAIR, off + (p + 1) * PAIR)
        zc = z_ref[0, :, sl]
        before = jnp.where(first, 0.0, zp_ref[0, 7:8, sl])
        after = jnp.where(last, 0.0, zn_ref[0, 0:1, sl])
        zm = jnp.where(row == 0, before, pltpu.roll(zc, 1, 0))
        zq = jnp.where(row == C - 1, after, pltpu.roll(zc, C - 1, 0))
        return zm * cw_ref[0:1, sl] + zc * cw_ref[1:2, sl] + zq * cw_ref[2:3, sl]

    P = range(pairs)
    sls = [slice(p * PAIR, (p + 1) * PAIR) for p in P]
    R = [conv(0, p) for p in P]
    Kraw = [conv(R_DIM, p) for p in P]
    V = [conv(2 * R_DIM, p) for p in P]
    kx = [Kraw[p] * kkg_ref[:, sls[p]] for p in P]
    KK = [kx[p] * lax.rsqrt(head_sum(kx[p] * kx[p]) + 1e-12) for p in P]
    A = [a_all[:, s] for s in sls]
    K = [Kraw[p] * (1.0 + (A[p] - 1.0) * ka_ref[:, sls[p]]) for p in P]
    Bv = [KK[p] * A[p] for p in P]
    for p in P:
        bon_ref[0, 0, :, sls[p]] = head_sum(R[p] * K[p] * rk_ref[:, sls[p]]) * V[p]
    At = [-KK[p] * e_prev[:, sls[p]] for p in P]
    Rt = [R[p] * e_cum[:, sls[p]] for p in P]
    bdv = [bd(V[p]) for p in P]
    sc = [_bdot(jnp.concatenate([At[p], Rt[p]], axis=0),
                jnp.concatenate([bd(Bv[p] * e_neg[:, sls[p]]), bd(K[p] * e_neg[:, sls[p]])], axis=0),
                _NT) for p in P]
    L = [jnp.where(strict, sc[p][:C, :PAIR], 0.0) for p in P]
    a_ak = [jnp.where(strict, sc[p][:C, PAIR:], 0.0) for p in P]
    p_rb = [jnp.where(incl, sc[p][C:, :PAIR], 0.0) for p in P]
    p_rk = [jnp.where(incl, sc[p][C:, PAIR:], 0.0) for p in P]
    akv = [_bdot(a_ak[p], bdv[p]) for p in P]
    x = [jnp.where(eye, 1.0, 0.0) + L[p] for p in P]
    lp = [_bdot(L[p], bd(L[p])) for p in P]
    n = 2
    while n < C:
        if 2 * n < C:
            o = [_bdot(lp[p], jnp.concatenate([bd(x[p]), bd(lp[p])], axis=1)) for p in P]
            x = [x[p] + o[p][:, :PAIR] for p in P]
            lp = [o[p][:, PAIR:] for p in P]
        else:
            x = [x[p] + _bdot(lp[p], bd(x[p])) for p in P]
        n *= 2
    z = [_bdot(x[p], jnp.concatenate([bd(At[p]), bd(akv[p])], axis=1)) for p in P]
    a_p = [z[p][:, :PAIR] for p in P]
    u_loc = [z[p][:, PAIR:] for p in P]
    f = [_bdot(p_rb[p], jnp.concatenate([bd(a_p[p]), bd(u_loc[p])], axis=1)) for p in P]
    y_rk = [_bdot(p_rk[p], bdv[p]) for p in P]
    b_d = [Bv[p] * e_end[:, sls[p]] for p in P]
    k_d = [K[p] * e_end[:, sls[p]] for p in P]
    g = [diag_blocks(_bdot(a_p[p], b_d[p], _TN)) for p in P]
    s_loc = [diag_blocks(_bdot(jnp.concatenate([u_loc[p], V[p]], axis=0),
                               jnp.concatenate([b_d[p], k_d[p]], axis=0), _TN)) for p in P]
    S = [s_scr[p] for p in P]
    for p in P:
        y_ref[0, 0, :, sls[p]] = (_bdot(Rt[p] + f[p][:, :PAIR], bd(S[p]), _NT)
                                  + (f[p][:, PAIR:] + y_rk[p]))
    for p in P:
        s_scr[p] = S[p] * e_tot[:, sls[p]] + _bdot(S[p], bd(g[p])) + s_loc[p]

    @pl.when(c == n_chunks - 1)
    def _():
        sf_ref[0, 0] = s_scr[...]


def _rwkv_scan(z_rkv, lora, p, s0):
    B, T, _ = z_rkv.shape
    C = SCAN_CHUNK
    nC = T // C
    n_pairs = H_RWKV // 2
    N = HEAD_RWKV
    hb = C // 8

    def to_pairs(s):
        return s.reshape(B, N_DIR, n_pairs, 2, N, N).swapaxes(3, 4).reshape(B, N_DIR, n_pairs, N, PAIR)

    def from_pairs(s):
        return s.reshape(B, N_DIR, n_pairs, N, 2, N).swapaxes(3, 4).reshape(B, N_DIR, H_RWKV, N, N)

    def tmap(dd, cc):
        return cc + dd * (nC - 1 - 2 * cc)

    row3 = lambda a: a.reshape(1, -1)
    per_dir_vec = pl.BlockSpec((1, 1, R_DIM), lambda dd, bb, cc: (dd, 0, 0))
    per_dir_mat = pl.BlockSpec((1, 128, R_DIM), lambda dd, bb, cc: (dd, 0, 0))
    vec = pl.BlockSpec((1, R_DIM), lambda dd, bb, cc: (0, 0))
    out_dir = pl.BlockSpec((1, 1, C, R_DIM), lambda dd, bb, cc: (dd, bb, tmap(dd, cc), 0))
    state = pl.BlockSpec((1, 1, n_pairs, N, PAIR), lambda dd, bb, cc: (bb, dd, 0, 0, 0))
    in_specs = [
        pl.BlockSpec((1, C, 3 * R_DIM), lambda dd, bb, cc: (bb, tmap(dd, cc), 0)),
        pl.BlockSpec((1, 8, 3 * R_DIM), lambda dd, bb, cc: (bb, jnp.maximum(tmap(dd, cc) * hb - 1, 0), 0)),
        pl.BlockSpec((1, 8, 3 * R_DIM),
                     lambda dd, bb, cc: (bb, jnp.minimum((tmap(dd, cc) + 1) * hb, T // 8 - 1), 0)),
        pl.BlockSpec((1, C, 128), lambda dd, bb, cc: (bb, tmap(dd, cc), dd)),
        pl.BlockSpec((3, 3 * R_DIM), lambda dd, bb, cc: (0, 0)),
        vec, vec, vec, per_dir_vec, per_dir_vec, per_dir_mat, per_dir_mat, state]
    y, bonus, s_fin = pl.pallas_call(
        functools.partial(_scan_kernel, chunk=C, pairs=n_pairs, n_chunks=nC),
        grid=(N_DIR, B, nC),
        in_specs=in_specs,
        out_specs=[out_dir, out_dir, state],
        out_shape=[jax.ShapeDtypeStruct((N_DIR, B, T, R_DIM), F32),
                   jax.ShapeDtypeStruct((N_DIR, B, T, R_DIM), F32),
                   jax.ShapeDtypeStruct((B, N_DIR, n_pairs, N, PAIR), F32)],
        scratch_shapes=[pltpu.VMEM((n_pairs, N, PAIR), F32)],
        compiler_params=_params(("parallel", "parallel", "arbitrary")),
        name="rwkv_scan",
    )(z_rkv, z_rkv, z_rkv, lora, p['conv_rkv'], row3(p['k_k']), row3(p['k_a']), row3(p['r_k']),
      p['w0'].reshape(N_DIR, 1, R_DIM), p['a0'].reshape(N_DIR, 1, R_DIM), p['w_up_pad'], p['a_up_pad'],
      to_pairs(s0))
    return y, bonus, from_pairs(s_fin)


def _post_kernel(y0_ref, y1_ref, b0_ref, b1_ref, gd_ref, gup_ref, lw_ref, lb_ref, o_ref):
    tm = o_ref.shape[0]
    g = jnp.dot(gd_ref[...].astype(BF16), gup_ref[...], preferred_element_type=F32)
    head0 = lax.broadcasted_iota(jnp.int32, (tm, PAIR), 1) < HEAD_RWKV

    def head_mean(x):
        s0 = jnp.sum(jnp.where(head0, x, 0.0), axis=-1, keepdims=True)
        s1 = jnp.sum(jnp.where(head0, 0.0, x), axis=-1, keepdims=True)
        return jnp.where(head0, s0, s1) * (1.0 / HEAD_RWKV)

    for p in range(R_DIM // PAIR):
        sl = slice(p * PAIR, (p + 1) * PAIR)
        ys = y0_ref[0, :, sl] + y1_ref[0, :, sl]
        dev = ys - head_mean(ys)
        yn = dev * lax.rsqrt(head_mean(dev * dev) + LNX_EPS)
        o = yn * lw_ref[:, sl] + lb_ref[:, sl] + (b0_ref[0, :, sl] + b1_ref[0, :, sl])
        o_ref[:, sl] = (o * g[:, sl]).astype(o_ref.dtype)


def _rwkv_post(y, bonus, lora, p, *, tm=256):
    _, M, Rd = y.shape
    tm = min(tm, M)
    assert M % tm == 0
    d0 = pl.BlockSpec((1, tm, Rd), lambda i: (0, i, 0))
    d1 = pl.BlockSpec((1, tm, Rd), lambda i: (1, i, 0))
    vec = pl.BlockSpec((1, Rd), lambda i: (0, 0))
    return pl.pallas_call(
        _post_kernel,
        grid=(M // tm,),
        in_specs=[d0, d1, d0, d1, pl.BlockSpec((tm, 128), lambda i: (i, 2)),
                  pl.BlockSpec((G_LORA, Rd), lambda i: (0, 0)), vec, vec],
        out_specs=pl.BlockSpec((tm, Rd), lambda i: (i, 0)),
        out_shape=jax.ShapeDtypeStruct((M, Rd), BF16),
        compiler_params=_params(("parallel",)),
        name="rwkv_post",
    )(y, y, bonus, bonus, lora, p['g_up_bf16'], p['lnx_w'].reshape(1, Rd), p['lnx_b'].reshape(1, Rd))


def _rope_tables(T, n_cache):
    rows = T // GRID_W
    row = jnp.repeat(jnp.arange(rows, dtype=F32), GRID_W)
    col = jnp.tile(jnp.arange(GRID_W, dtype=F32), rows)
    inv_freq = jnp.power(ROPE_THETA, -jnp.arange(AXIS_PAIRS, dtype=F32) / AXIS_PAIRS)
    ar, ac = row[:, None] * inv_freq, col[:, None] * inv_freq
    cos = jnp.concatenate([jnp.cos(ar), jnp.cos(ar), jnp.cos(ac), jnp.cos(ac)], axis=1)
    sin = jnp.concatenate([-jnp.sin(ar), jnp.sin(ar), -jnp.sin(ac), jnp.sin(ac)], axis=1)
    cos = jnp.concatenate([cos, jnp.ones((n_cache, ROPE_DIM), F32)], axis=0)
    sin = jnp.concatenate([sin, jnp.zeros((n_cache, ROPE_DIM), F32)], axis=0)
    return jnp.tile(cos, (1, 2)), jnp.tile(sin, (1, 2))


def _trunk_layer(x, mod, p, cache):
    B, T, _ = x.shape
    M = B * T
    nb = mod.shape[0]
    shift1, scale1, gate1, shift2, scale2, gate2 = [
        m.reshape(nb, 1, D_MODEL) for m in jnp.split(mod, 6, axis=-1)]
    xf = x.reshape(M, D_MODEL)
    latent = cache is not None
    proj = functools.partial(_norm_matmul, xf, p['norm1'], scale1, shift1, rows_per_batch=T)

    q_tm = 512
    extra, extra_specs = [p['q_gain']], [lambda tm: pl.BlockSpec((1, 2 * Q_PAIR), lambda i, j: (0, 0))]
    if latent:
        cos, sin = _rope_tables(T, cache[0].shape[1])
        n_t = T // min(q_tm, T)
        tbl = lambda tm: pl.BlockSpec((tm, 128), lambda i, j: (i % n_t, 0))
        extra += [cos[:T], sin[:T]]
        extra_specs += [tbl, tbl]
    q, = proj(p['w_q'], tm=q_tm, tn=2 * Q_PAIR, epilogue=functools.partial(_epi_q, rope=latent),
              outs=[(2 * Q_PAIR, Q_DIM, BF16)], extra=extra, extra_specs=extra_specs, name="proj_q")
    z_ckv, z_kr, lora = proj(p['w_small'], tm=1024, tn=SMALL_COLS, epilogue=_epi_small,
                             outs=[(KV_RANK, KV_RANK, F32), (ROPE_DIM, ROPE_DIM, F32), (384, 384, F32)],
                             name="proj_small")
    z_rkv, = proj(p['w_rkv'], tm=1024, tn=512, epilogue=_epi_plain,
                  outs=[(512, 3 * R_DIM, F32)], name="proj_rkv")
    gates, = proj(p['w_gate'], tm=1024, tn=512, epilogue=_epi_sigmoid,
                  outs=[(512, 2 * D_MODEL, BF16)], name="proj_gate")

    if latent:
        ckv_ctx, kr_ctx, s0 = cache
        S = T + ckv_ctx.shape[1]
        ckv_all = jnp.concatenate([z_ckv.reshape(B, T, KV_RANK), ckv_ctx], axis=1).reshape(B * S, KV_RANK)
        kr_all = jnp.concatenate([z_kr.reshape(B, T, ROPE_DIM), kr_ctx], axis=1).reshape(B * S, ROPE_DIM)
        k, v = _keys_values(ckv_all, kr_all, p, (cos, sin), tm=S)
    else:
        S = T
        s0 = jnp.zeros((B, N_DIR, H_RWKV, HEAD_RWKV, HEAD_RWKV), F32)
        k, v = _keys_values(z_ckv, z_kr, p, None, tm=1024)
    o_mla = _attention(q.reshape(B, T, Q_DIM), k, v, S).reshape(M, MLA_WIDTH)

    y, bonus, s_final = _rwkv_scan(z_rkv.reshape(B, T, 3 * R_DIM), lora.reshape(B, T, 384), p, s0)
    o_rwkv = _rwkv_post(y.reshape(N_DIR, M, R_DIM), bonus.reshape(N_DIR, M, R_DIM), lora, p)

    merged = _branch_merge(o_mla, o_rwkv, gates, p['w_br_mla'], p['w_br_rwkv'])
    x1 = _out_proj(merged, p['w_out'], xf, gate1, T)
    x2 = _ffn(x1, p['norm2'], scale2, shift2, gate2, p['w_ff_in'], p['w_ff_out'], T)
    return (x2.reshape(B, T, D_MODEL),
            (z_ckv.reshape(B, T, KV_RANK), z_kr.reshape(B, T, ROPE_DIM), s_final))


def _split_w_in(w):
    offs = [sum(IN_SIZES[:i]) for i in range(len(IN_SIZES))]
    part = lambda i: w[:, offs[i]:offs[i] + IN_SIZES[i]]
    K = w.shape[0]
    wq = part(0).reshape(K, H_MLA // 2, 2, QK_HEAD)
    wq = jnp.concatenate([wq[..., :NOPE_DIM].reshape(K, H_MLA // 2, 2 * NOPE_DIM),
                          wq[..., NOPE_DIM:].reshape(K, H_MLA // 2, 2 * ROPE_DIM)], axis=-1)
    wd, ad = part(4), part(5)
    small = jnp.concatenate([part(1), part(2), jnp.zeros((K, 128 - ROPE_DIM), w.dtype),
                             wd[:, :W_LORA], ad[:, :A_LORA], wd[:, W_LORA:], ad[:, A_LORA:],
                             part(6)], axis=1)
    return (wq.reshape(K, Q_DIM).astype(BF16), small.astype(BF16), part(3).astype(BF16),
            part(7).astype(BF16))


def _layer_params(l, w_in, q_norm, w_kv_up, w_br_mla, w_br_rwkv, w_out, w_ff_in, w_ff_out, **small):
    zeros = jnp.zeros((N_DIR, W_LORA, R_DIM), F32)
    w_q, w_small, w_rkv, w_gate = _split_w_in(w_in[l])
    qg = q_norm[l] * (QK_HEAD ** -0.5)
    q_gain = jnp.concatenate([qg[:NOPE_DIM], qg[:NOPE_DIM], qg[NOPE_DIM:], qg[NOPE_DIM:]])
    p = {name: val[l] for name, val in small.items()}
    p.update({
        'w_q': w_q, 'w_small': w_small, 'w_rkv': w_rkv, 'w_gate': w_gate,
        'q_gain': jnp.tile(q_gain, 2).reshape(1, 2 * Q_PAIR),
        'w_kv_up': w_kv_up[l].astype(BF16),
        'w_up_pad': jnp.concatenate([small['w_up'][l], zeros], axis=1).astype(BF16),
        'a_up_pad': jnp.concatenate([zeros, small['a_up'][l]], axis=1).astype(BF16),
        'g_up_bf16': small['g_up'][l].astype(BF16),
        'w_br_mla': w_br_mla[l].astype(BF16), 'w_br_rwkv': w_br_rwkv[l].astype(BF16),
        'w_out': w_out[l].astype(BF16),
        'w_ff_in': w_ff_in[l].astype(BF16), 'w_ff_out': w_ff_out[l].astype(BF16),
    })
    return p


def kernel(x_prompt, x_sample, cache_mla_ckv, cache_mla_kr, state_rwkv, c, c_ctx,
           norm1, w_ada, b_ada, w_in, q_norm, kv_norm, w_kv_up, k_norm, conv_rkv,
           k_k, k_a, r_k, w0, w_up, a0, a_up, g_up, lnx_w, lnx_b,
           w_br_mla, w_br_rwkv, w_out, norm2, w_ff_in, w_ff_out):
    x_p, x_s = x_prompt, x_sample
    n_lat = c.shape[0]
    ckv_list, kr_list, st_list = [], [], []
    for l in range(DEPTH):
        p = _layer_params(l, w_in, q_norm, w_kv_up, w_br_mla, w_br_rwkv, w_out, w_ff_in, w_ff_out,
                          norm1=norm1, kv_norm=kv_norm, k_norm=k_norm, conv_rkv=conv_rkv, k_k=k_k,
                          k_a=k_a, r_k=r_k, w0=w0, w_up=w_up, a0=a0, a_up=a_up, g_up=g_up,
                          lnx_w=lnx_w, lnx_b=lnx_b, norm2=norm2)
        cond = jnp.concatenate([c, c_ctx[None]], axis=0)
        cond = jnp.pad(jax.nn.silu(cond), ((0, (-cond.shape[0]) % 16), (0, 0)))
        mod = _matmul(cond, w_ada[l], tm=16, tn=1024)[:n_lat + 1] + b_ada[l]
        x_p, (ckv_l, kr_l, st_l) = _trunk_layer(x_p, mod[n_lat:], p, None)
        ckv_list.append(ckv_l)
        kr_list.append(kr_l)
        st_list.append(st_l)
        x_s, _ = _trunk_layer(x_s, mod[:n_lat], p,
                              (cache_mla_ckv[:, l], cache_mla_kr[:, l], state_rwkv[:, l]))
    return (x_p, x_s, jnp.stack(ckv_list, axis=1), jnp.stack(kr_list, axis=1),
            jnp.stack(st_list, axis=1))
```

```python
import functools

import jax
import jax.numpy as jnp
from jax import lax
from jax.experimental import pallas as pl
from jax.experimental.pallas import tpu as pltpu

D_MODEL = 2048
DEPTH = 1
GRID_W = 64
H_MLA = 16
NOPE_DIM = 128
ROPE_DIM = 64
QK_HEAD = NOPE_DIM + ROPE_DIM
V_HEAD = 128
KV_RANK = 512
ROPE_THETA = 10000.0
AXIS_DIM = ROPE_DIM // 2
AXIS_PAIRS = AXIS_DIM // 2
MLA_WIDTH = H_MLA * V_HEAD
Q_DIM = H_MLA * QK_HEAD
HEAD_RWKV = 64
H_RWKV = D_MODEL // HEAD_RWKV
R_DIM = H_RWKV * HEAD_RWKV
W_LORA = 64
A_LORA = 64
G_LORA = 128
N_DIR = 2
LNX_EPS = 64e-5
D_FF = 4 * D_MODEL
EPS = 1e-6
IN_SIZES = (Q_DIM, KV_RANK, ROPE_DIM, 3 * R_DIM, N_DIR * W_LORA, N_DIR * A_LORA, G_LORA, 2 * D_MODEL)
IN_DIM = sum(IN_SIZES)

F32 = jnp.float32
BF16 = jnp.bfloat16
VMEM_LIMIT = 56 * 1024 * 1024
SCAN_CHUNK = 64
SCAN_PAIRS = 16


def _params(sem):
    return pltpu.CompilerParams(dimension_semantics=sem, vmem_limit_bytes=VMEM_LIMIT)


def _mm_kernel(a_ref, b_ref, o_ref):
    o_ref[...] = jnp.dot(a_ref[...].astype(BF16), b_ref[...],
                         preferred_element_type=F32).astype(o_ref.dtype)


def _matmul(a, b, *, tm=512, tn=512, out_dtype=F32):
    M, K = a.shape
    N = b.shape[1]
    tm = min(tm, M)
    tn = min(tn, N)
    assert M % tm == 0 and N % tn == 0, (M, N, tm, tn)
    return pl.pallas_call(
        _mm_kernel,
        grid=(M // tm, N // tn),
        in_specs=[pl.BlockSpec((tm, K), lambda i, j: (i, 0)),
                  pl.BlockSpec((K, tn), lambda i, j: (0, j))],
        out_specs=pl.BlockSpec((tm, tn), lambda i, j: (i, j)),
        out_shape=jax.ShapeDtypeStruct((M, N), out_dtype),
        compiler_params=_params(("parallel", "arbitrary")),
        name="matmul",
    )(a, b.astype(BF16))


def _modulated_norm(x, g, scale, shift):
    ms = jnp.mean(x * x, axis=-1, keepdims=True)
    return (x * lax.rsqrt(ms + EPS) * g) * (1.0 + scale) + shift


def _mod_index(nb, rows_per_batch, tm):
    if nb == 1:
        return lambda i, j: (0, 0, 0)
    assert rows_per_batch % tm == 0
    return lambda i, j: (i * tm // rows_per_batch, 0, 0)


def _row_tile(tm, M, nb, rows_per_batch):
    tm = min(tm, M, rows_per_batch) if nb > 1 else min(tm, M)
    assert M % tm == 0
    return tm


def _sigmoid(x):
    return 1.0 / (1.0 + jnp.exp(-x))


def _swap16(x):
    lane = lax.broadcasted_iota(jnp.int32, x.shape, 1)
    return jnp.where((lane & AXIS_PAIRS) == 0,
                     pltpu.roll(x, 128 - AXIS_PAIRS, 1), pltpu.roll(x, AXIS_PAIRS, 1))


def _epi_plain(acc, extra, outs, rows):
    outs[0][rows, :] = acc.astype(outs[0].dtype)


def _epi_sigmoid(acc, extra, outs, rows):
    outs[0][rows, :] = _sigmoid(acc).astype(outs[0].dtype)


SMALL_LORA = KV_RANK + 128
SMALL_COLS = SMALL_LORA + 3 * 128


def _epi_small(acc, extra, outs, rows):
    ckv_ref, kr_ref, lora_ref = outs
    ckv_ref[rows, :] = acc[:, :KV_RANK]
    kr_ref[rows, :] = acc[:, KV_RANK:KV_RANK + ROPE_DIM]
    o = SMALL_LORA
    wa = acc[:, o:o + 256]
    lane = lax.broadcasted_iota(jnp.int32, wa.shape, 1)
    wa = jnp.where((lane & W_LORA) == 0, jnp.tanh(wa), wa)
    lora_ref[rows, :] = jnp.concatenate([wa, _sigmoid(acc[:, o + 256:o + 384])], axis=1)


Q_PAIR = 2 * QK_HEAD


def _epi_q(acc, extra, outs, rows, *, rope):
    g_ref = extra[0]
    tm, tn = acc.shape
    lo = lax.broadcasted_iota(jnp.int32, (tm, 128), 1) < ROPE_DIM
    parts = []
    for p in range(tn // Q_PAIR):
        z = acc[:, p * Q_PAIR:(p + 1) * Q_PAIR]
        g = g_ref[:, p * Q_PAIR:(p + 1) * Q_PAIR]
        n0, n1, rp = z[:, :128], z[:, 128:256], z[:, 256:]
        rp2 = rp * rp
        s0 = (jnp.sum(n0 * n0, axis=-1, keepdims=True)
              + jnp.sum(jnp.where(lo, rp2, 0.0), axis=-1, keepdims=True))
        s1 = (jnp.sum(n1 * n1, axis=-1, keepdims=True)
              + jnp.sum(jnp.where(lo, 0.0, rp2), axis=-1, keepdims=True))
        r0 = lax.rsqrt(s0 * (1.0 / QK_HEAD) + EPS)
        r1 = lax.rsqrt(s1 * (1.0 / QK_HEAD) + EPS)
        qr = rp * jnp.where(lo, r0, r1) * g[:, 256:]
        if rope:
            qr = qr * extra[1][rows, :] + _swap16(qr) * extra[2][rows, :]
        parts += [n0 * r0 * g[:, :128], n1 * r1 * g[:, 128:256], qr]
    outs[0][rows, :] = jnp.concatenate(parts, axis=1).astype(outs[0].dtype)


def _norm_mm_kernel(*refs, n_extra, n_out, epilogue, row_split):
    x_ref, g_ref, sc_ref, sh_ref, w_ref = refs[:5]
    extra = refs[5:5 + n_extra]
    outs = refs[5 + n_extra:5 + n_extra + n_out]
    h_scr = refs[-1]

    @pl.when(pl.program_id(1) == 0)
    def _():
        h = _modulated_norm(x_ref[...], g_ref[...], sc_ref[0], sh_ref[0])
        h_scr[...] = h.astype(BF16)

    step = h_scr.shape[0] // row_split
    for r in range(row_split):
        rows = pl.ds(r * step, step)
        epilogue(jnp.dot(h_scr[rows, :], w_ref[...], preferred_element_type=F32), extra, outs, rows)


def _norm_matmul(x, g, scale, shift, w, rows_per_batch, *, tm, tn, epilogue, outs,
                 extra=(), extra_specs=(), row_split=2, name):
    M, K = x.shape
    N = w.shape[1]
    nb = scale.shape[0]
    tm = _row_tile(tm, M, nb, rows_per_batch)
    assert N % tn == 0 and tm % (16 * row_split) == 0
    midx = _mod_index(nb, rows_per_batch, tm)
    res = pl.pallas_call(
        functools.partial(_norm_mm_kernel, n_extra=len(extra), n_out=len(outs), epilogue=epilogue,
                          row_split=row_split),
        grid=(M // tm, N // tn),
        in_specs=[pl.BlockSpec((tm, K), lambda i, j: (i, 0)),
                  pl.BlockSpec((1, K), lambda i, j: (0, 0)),
                  pl.BlockSpec((1, 1, K), midx),
                  pl.BlockSpec((1, 1, K), midx),
                  pl.BlockSpec((K, tn), lambda i, j: (0, j))] + [s(tm) for s in extra_specs],
        out_specs=[pl.BlockSpec((tm, c), lambda i, j: (i, j)) for c, _, _ in outs],
        out_shape=[jax.ShapeDtypeStruct((M, n), dt) for _, n, dt in outs],
        scratch_shapes=[pltpu.VMEM((tm, K), BF16)],
        compiler_params=_params(("parallel", "arbitrary")),
        name=name,
    )(x, g.reshape(1, K), scale, shift, w, *extra)
    return res


K_WIDTH = 2 * NOPE_DIM
KV_ROW_SPLIT = 4


def _kv_kernel(*refs, rope):
    if rope:
        (ckv_ref, kr_ref, gkv_ref, gk_ref, cos_ref, sin_ref, w_ref, k_ref, v_ref,
         a_scr, kr_scr, ss_scr) = refs
    else:
        ckv_ref, kr_ref, gkv_ref, gk_ref, w_ref, k_ref, v_ref, a_scr, kr_scr, ss_scr = refs
    h = pl.program_id(1)

    @pl.when(h == 0)
    def _():
        c = ckv_ref[...]
        a = c * lax.rsqrt(jnp.mean(c * c, axis=-1, keepdims=True) + EPS) * gkv_ref[...]
        a_scr[...] = a.astype(BF16)
        kr = kr_ref[...]
        ss_scr[...] = jnp.broadcast_to(0.5 * jnp.sum(kr * kr, axis=-1, keepdims=True), ss_scr.shape)
        krg = kr * gk_ref[:, NOPE_DIM:]
        if rope:
            krg = krg * cos_ref[...] + _swap16(krg) * sin_ref[...]
        kr_scr[...] = krg

    step = a_scr.shape[0] // KV_ROW_SPLIT
    for r in range(KV_ROW_SPLIT):
        rows = pl.ds(r * step, step)
        acc = jnp.dot(a_scr[rows, :], w_ref[...], preferred_element_type=F32)
        kn = acc[:, :NOPE_DIM]
        ssq = jnp.sum(kn * kn, axis=-1, keepdims=True) + ss_scr[rows, :1]
        rstd = lax.rsqrt(ssq * (1.0 / QK_HEAD) + EPS)
        lane = lax.broadcasted_iota(jnp.int32, kn.shape, 1)
        mine = (lane >> 6) == (h & 1)
        krh = jnp.where(mine, kr_scr[rows, :] * rstd, 0.0)
        k_ref[0, rows, :] = jnp.concatenate([kn * rstd * gk_ref[:, :NOPE_DIM], krh], axis=1).astype(BF16)
        v_ref[0, rows, :] = acc[:, NOPE_DIM:].astype(BF16)


def _keys_values(ckv, kr, p, tables, *, tm):
    Mk = ckv.shape[0]
    tm = min(tm, Mk)
    assert Mk % tm == 0 and tm % (16 * KV_ROW_SPLIT) == 0
    rope = tables is not None
    kr2 = jnp.concatenate([kr, kr], axis=-1)
    gk = jnp.concatenate([p['k_norm'], p['k_norm'][NOPE_DIM:]]).reshape(1, K_WIDTH)
    row = lambda i, h: (i, 0)
    fixed = lambda i, h: (0, 0)
    in_specs = [pl.BlockSpec((tm, KV_RANK), row), pl.BlockSpec((tm, 128), row),
                pl.BlockSpec((1, KV_RANK), fixed), pl.BlockSpec((1, K_WIDTH), fixed)]
    args = [ckv, kr2, p['kv_norm'].reshape(1, KV_RANK), gk]
    if rope:
        in_specs += [pl.BlockSpec((tm, 128), fixed), pl.BlockSpec((tm, 128), fixed)]
        args += list(tables)
    in_specs.append(pl.BlockSpec((KV_RANK, NOPE_DIM + V_HEAD), lambda i, h: (0, h)))
    args.append(p['w_kv_up'])
    return pl.pallas_call(
        functools.partial(_kv_kernel, rope=rope),
        grid=(Mk // tm, H_MLA),
        in_specs=in_specs,
        out_specs=[pl.BlockSpec((1, tm, K_WIDTH), lambda i, h: (h, i, 0)),
                   pl.BlockSpec((1, tm, V_HEAD), lambda i, h: (h, i, 0))],
        out_shape=[jax.ShapeDtypeStruct((H_MLA, Mk, K_WIDTH), BF16),
                   jax.ShapeDtypeStruct((H_MLA, Mk, V_HEAD), BF16)],
        scratch_shapes=[pltpu.VMEM((tm, KV_RANK), BF16), pltpu.VMEM((tm, 128), F32),
                        pltpu.VMEM((tm, 128), F32)],
        compiler_params=_params(("parallel", "arbitrary")),
        name="keys_values",
    )(*args)


def _attn_kernel(q_ref, k_ref, v_ref, o_ref, *, pairs):
    for pr in range(pairs):
        q = q_ref[0, :, pr * Q_PAIR:(pr + 1) * Q_PAIR]
        qr = q[:, 2 * NOPE_DIM:]
        for h in range(2):
            hh = 2 * pr + h
            qh = jnp.concatenate([q[:, h * NOPE_DIM:(h + 1) * NOPE_DIM], qr], axis=1)
            s = lax.dot_general(qh, k_ref[hh], (((1,), (1,)), ((), ())), preferred_element_type=F32)
            m = jnp.max(s, axis=-1, keepdims=True)
            p = jnp.exp(s - m)
            l = jnp.sum(p, axis=-1, keepdims=True)
            o = jnp.dot(p.astype(BF16), v_ref[hh], preferred_element_type=F32) / l
            o_ref[0, :, hh * V_HEAD:(hh + 1) * V_HEAD] = o.astype(o_ref.dtype)


def _attention(q, k, v, S, *, tq=256, pairs):
    B, T, _ = q.shape
    tq = min(tq, T)
    hp = 2 * pairs
    return pl.pallas_call(
        functools.partial(_attn_kernel, pairs=pairs),
        grid=(B, H_MLA // hp, T // tq),
        in_specs=[pl.BlockSpec((1, tq, pairs * Q_PAIR), lambda b, p, i: (b, i, p)),
                  pl.BlockSpec((hp, S, K_WIDTH), lambda b, p, i: (p, b, 0)),
                  pl.BlockSpec((hp, S, V_HEAD), lambda b, p, i: (p, b, 0))],
        out_specs=pl.BlockSpec((1, tq, hp * V_HEAD), lambda b, p, i: (b, i, p)),
        out_shape=jax.ShapeDtypeStruct((B, T, MLA_WIDTH), BF16),
        compiler_params=_params(("parallel", "parallel", "arbitrary")),
        name="attention",
    )(q, k, v)


def _branch_kernel(a_ref, b_ref, ga_ref, gb_ref, wa_ref, wb_ref, o_ref):
    ya = jnp.dot(a_ref[...], wa_ref[...], preferred_element_type=F32)
    yb = jnp.dot(b_ref[...], wb_ref[...], preferred_element_type=F32)
    o_ref[...] = (ga_ref[...].astype(F32) * ya + gb_ref[...].astype(F32) * yb).astype(o_ref.dtype)


def _branch_merge(a, b, gates, wa, wb, *, tm=1024, tn=512):
    M, K = a.shape
    N = wa.shape[1]
    tm = min(tm, M)
    nj = N // tn
    return pl.pallas_call(
        _branch_kernel,
        grid=(M // tm, nj),
        in_specs=[pl.BlockSpec((tm, K), lambda i, j: (i, 0)),
                  pl.BlockSpec((tm, K), lambda i, j: (i, 0)),
                  pl.BlockSpec((tm, tn), lambda i, j: (i, j)),
                  pl.BlockSpec((tm, tn), lambda i, j: (i, j + nj)),
                  pl.BlockSpec((K, tn), lambda i, j: (0, j)),
                  pl.BlockSpec((K, tn), lambda i, j: (0, j))],
        out_specs=pl.BlockSpec((tm, tn), lambda i, j: (i, j)),
        out_shape=jax.ShapeDtypeStruct((M, N), BF16),
        compiler_params=_params(("parallel", "arbitrary")),
        name="branch_merge",
    )(a, b, gates, gates, wa, wb)


def _resid_kernel(m_ref, w_ref, x_ref, gt_ref, o_ref):
    o_ref[...] = x_ref[...] + gt_ref[0] * jnp.dot(m_ref[...], w_ref[...],
                                                  preferred_element_type=F32)


def _out_proj(m, w, x, gate, rows_per_batch, *, tm=1024, tn=512):
    M, K = m.shape
    N = w.shape[1]
    nb = gate.shape[0]
    tm = _row_tile(tm, M, nb, rows_per_batch)
    if nb == 1:
        gidx = lambda i, j: (0, 0, j)
    else:
        gidx = lambda i, j: (i * tm // rows_per_batch, 0, j)
    return pl.pallas_call(
        _resid_kernel,
        grid=(M // tm, N // tn),
        in_specs=[pl.BlockSpec((tm, K), lambda i, j: (i, 0)),
                  pl.BlockSpec((K, tn), lambda i, j: (0, j)),
                  pl.BlockSpec((tm, tn), lambda i, j: (i, j)),
                  pl.BlockSpec((1, 1, tn), gidx)],
        out_specs=pl.BlockSpec((tm, tn), lambda i, j: (i, j)),
        out_shape=jax.ShapeDtypeStruct((M, N), F32),
        compiler_params=_params(("parallel", "arbitrary")),
        name="out_proj",
    )(m, w, x, gate)


def _ffn_kernel(x_ref, g_ref, sc_ref, sh_ref, gt_ref, w1_ref, w2_ref, o_ref, h_scr, acc_scr):
    f = pl.program_id(1)

    @pl.when(f == 0)
    def _():
        h = _modulated_norm(x_ref[...], g_ref[...], sc_ref[0], sh_ref[0])
        h_scr[...] = h.astype(BF16)
        acc_scr[...] = jnp.zeros_like(acc_scr)

    u = jnp.dot(h_scr[...], w1_ref[...], preferred_element_type=F32)
    u = jnp.square(jnp.maximum(u, 0.0))
    acc_scr[...] += jnp.dot(u.astype(BF16), w2_ref[...], preferred_element_type=F32)

    @pl.when(f == pl.num_programs(1) - 1)
    def _():
        o_ref[...] = x_ref[...] + gt_ref[0] * acc_scr[...]


def _ffn(x, g, scale, shift, gate, w1, w2, rows_per_batch, *, tm=512, tf=512):
    M, K = x.shape
    F = w1.shape[1]
    nb = scale.shape[0]
    tm = _row_tile(tm, M, nb, rows_per_batch)
    assert F % tf == 0
    midx = _mod_index(nb, rows_per_batch, tm)
    return pl.pallas_call(
        _ffn_kernel,
        grid=(M // tm, F // tf),
        in_specs=[pl.BlockSpec((tm, K), lambda i, j: (i, 0)),
                  pl.BlockSpec((1, K), lambda i, j: (0, 0)),
                  pl.BlockSpec((1, 1, K), midx),
                  pl.BlockSpec((1, 1, K), midx),
                  pl.BlockSpec((1, 1, K), midx),
                  pl.BlockSpec((K, tf), lambda i, j: (0, j)),
                  pl.BlockSpec((tf, K), lambda i, j: (j, 0))],
        out_specs=pl.BlockSpec((tm, K), lambda i, j: (i, 0)),
        out_shape=jax.ShapeDtypeStruct((M, K), F32),
        scratch_shapes=[pltpu.VMEM((tm, K), BF16), pltpu.VMEM((tm, K), F32)],
        compiler_params=_params(("parallel", "arbitrary")),
        name="ffn",
    )(x, g.reshape(1, K), scale, shift, gate, w1, w2)


def _bdot(a, b, dims=((1,), (0,))):
    return lax.dot_general(a.astype(BF16), b.astype(BF16), (dims, ((), ())),
                           preferred_element_type=F32)


_NT = ((1,), (1,))
_TN = ((0,), (0,))
PAIR = 2 * HEAD_RWKV


def _scan_kernel(z_ref, zp_ref, zn_ref, lo_ref, cw_ref, kkg_ref, ka_ref, rk_ref, w0_ref, a0_ref,
                 wup_ref, aup_ref, s0_ref, y_ref, bon_ref, sf_ref, s_scr, *, chunk, pairs, n_chunks):
    d = pl.program_id(0)
    c = pl.program_id(2)
    C = chunk
    N = HEAD_RWKV

    @pl.when(c == 0)
    def _():
        for p in range(pairs):
            s_scr[p] = jnp.concatenate([s0_ref[0, 0, 2 * p], s0_ref[0, 0, 2 * p + 1]], axis=1)

    sgn = 1 - 2 * d
    tt = c + d * (n_chunks - 1 - 2 * c)
    row = lax.broadcasted_iota(jnp.int32, (C, PAIR), 0)
    lane = lax.broadcasted_iota(jnp.int32, (C, PAIR), 1)
    col = lane & (N - 1)
    rel = (row - col) * sgn
    incl = rel >= 0
    strict = rel > 0
    eye = row == col
    head0 = lane < N
    ti = lax.broadcasted_iota(jnp.int32, (C, C), 0)
    tj = lax.broadcasted_iota(jnp.int32, (C, C), 1)
    tri = jnp.where((ti - tj) * sgn >= 0, 1.0, 0.0).astype(BF16)

    def bd(x):
        return jnp.concatenate([jnp.where(head0, x, 0.0), jnp.where(head0, 0.0, x)],
                               axis=0).astype(BF16)

    def diag_blocks(x):
        return jnp.where(head0, x[:N], x[N:])

    def head_sum(x):
        s0 = jnp.sum(jnp.where(head0, x, 0.0), axis=-1, keepdims=True)
        s1 = jnp.sum(jnp.where(head0, 0.0, x), axis=-1, keepdims=True)
        return jnp.where(head0, s0, s1)

    lora = lo_ref[0].astype(BF16)
    u = w0_ref[0] + jnp.dot(lora, wup_ref[0], preferred_element_type=F32)
    softplus = jnp.maximum(-u, 0.0) + jnp.log(1.0 + jnp.exp(-jnp.abs(u)))
    ld = -jnp.exp(-softplus - 0.5)
    a_all = _sigmoid(a0_ref[0] + jnp.dot(lora, aup_ref[0], preferred_element_type=F32))

    ld_hi = ld.astype(BF16)
    rem = ld - ld_hi.astype(F32)
    ld_mid = rem.astype(BF16)
    ld_lo = (rem - ld_mid.astype(F32)).astype(BF16)
    cum = (jnp.dot(tri, ld_hi, preferred_element_type=F32)
           + jnp.dot(tri, ld_mid, preferred_element_type=F32)
           + jnp.dot(tri, ld_lo, preferred_element_type=F32))
    tot = jnp.sum(ld, axis=0, keepdims=True)
    e_cum = jnp.exp(cum)
    e_prev = jnp.exp(cum - ld)
    e_neg = jnp.exp(-cum)
    e_end = jnp.exp(tot - cum)
    e_tot = jnp.exp(tot)

    first = tt == 0
    last = tt == n_chunks - 1

    def conv(off, p):
        sl = slice(off + p * PAIR, off + (p + 1) * PAIR)
        zc = z_ref[0, :, sl]
        before = jnp.where(first, 0.0, zp_ref[0, 7:8, sl])
        after = jnp.where(last, 0.0, zn_ref[0, 0:1, sl])
        zm = jnp.where(row == 0, before, pltpu.roll(zc, 1, 0))
        zq = jnp.where(row == C - 1, after, pltpu.roll(zc, C - 1, 0))
        return zm * cw_ref[0:1, sl] + zc * cw_ref[1:2, sl] + zq * cw_ref[2:3, sl]

    P = range(pairs)
    sls = [slice(p * PAIR, (p + 1) * PAIR) for p in P]
    R = [conv(0, p) for p in P]
    Kraw = [conv(R_DIM, p) for p in P]
    V = [conv(2 * R_DIM, p) for p in P]
    kx = [Kraw[p] * kkg_ref[:, sls[p]] for p in P]
    KK = [kx[p] * lax.rsqrt(head_sum(kx[p] * kx[p]) + 1e-12) for p in P]
    A = [a_all[:, s] for s in sls]
    K = [Kraw[p] * (1.0 + (A[p] - 1.0) * ka_ref[:, sls[p]]) for p in P]
    Bv = [KK[p] * A[p] for p in P]
    for p in P:
        bon_ref[0, 0, :, sls[p]] = head_sum(R[p] * K[p] * rk_ref[:, sls[p]]) * V[p]
    At = [-KK[p] * e_prev[:, sls[p]] for p in P]
    Rt = [R[p] * e_cum[:, sls[p]] for p in P]
    bdv = [bd(V[p]) for p in P]
    sc = [_bdot(jnp.concatenate([At[p], Rt[p]], axis=0),
                jnp.concatenate([bd(Bv[p] * e_neg[:, sls[p]]), bd(K[p] * e_neg[:, sls[p]])], axis=0),
                _NT) for p in P]
    L = [jnp.where(strict, sc[p][:C, :PAIR], 0.0) for p in P]
    a_ak = [jnp.where(strict, sc[p][:C, PAIR:], 0.0) for p in P]
    p_rb = [jnp.where(incl, sc[p][C:, :PAIR], 0.0) for p in P]
    p_rk = [jnp.where(incl, sc[p][C:, PAIR:], 0.0) for p in P]
    akv = [_bdot(a_ak[p], bdv[p]) for p in P]
    x = [jnp.where(eye, 1.0, 0.0) + L[p] for p in P]
    lp = [_bdot(L[p], bd(L[p])) for p in P]
    n = 2
    while n < C:
        if 2 * n < C:
            o = [_bdot(lp[p], jnp.concatenate([bd(x[p]), bd(lp[p])], axis=1)) for p in P]
            x = [x[p] + o[p][:, :PAIR] for p in P]
            lp = [o[p][:, PAIR:] for p in P]
        else:
            x = [x[p] + _bdot(lp[p], bd(x[p])) for p in P]
        n *= 2
    z = [_bdot(x[p], jnp.concatenate([bd(At[p]), bd(akv[p])], axis=1)) for p in P]
    a_p = [z[p][:, :PAIR] for p in P]
    u_loc = [z[p][:, PAIR:] for p in P]
    f = [_bdot(p_rb[p], jnp.concatenate([bd(a_p[p]), bd(u_loc[p])], axis=1)) for p in P]
    y_rk = [_bdot(p_rk[p], bdv[p]) for p in P]
    b_d = [Bv[p] * e_end[:, sls[p]] for p in P]
    k_d = [K[p] * e_end[:, sls[p]] for p in P]
    g = [diag_blocks(_bdot(a_p[p], b_d[p], _TN)) for p in P]
    s_loc = [diag_blocks(_bdot(jnp.concatenate([u_loc[p], V[p]], axis=0),
                               jnp.concatenate([b_d[p], k_d[p]], axis=0), _TN)) for p in P]
    S = [s_scr[p] for p in P]
    for p in P:
        y_ref[0, 0, :, sls[p]] = (_bdot(Rt[p] + f[p][:, :PAIR], bd(S[p]), _NT)
                                  + (f[p][:, PAIR:] + y_rk[p]))
    for p in P:
        s_scr[p] = S[p] * e_tot[:, sls[p]] + _bdot(S[p], bd(g[p])) + s_loc[p]

    @pl.when(c == n_chunks - 1)
    def _():
        for p in range(pairs):
            sf_ref[0, 0, 2 * p] = s_scr[p, :, :N]
            sf_ref[0, 0, 2 * p + 1] = s_scr[p, :, N:]


def _rwkv_scan(z_rkv, lora, p, s0):
    B, T, _ = z_rkv.shape
    C = SCAN_CHUNK
    nC = T // C
    n_pairs = H_RWKV // 2
    N = HEAD_RWKV
    hb = C // 8

    def tmap(dd, cc):
        return cc + dd * (nC - 1 - 2 * cc)

    row3 = lambda a: a.reshape(1, -1)
    per_dir_vec = pl.BlockSpec((1, 1, R_DIM), lambda dd, bb, cc: (dd, 0, 0))
    per_dir_mat = pl.BlockSpec((1, 128, R_DIM), lambda dd, bb, cc: (dd, 0, 0))
    vec = pl.BlockSpec((1, R_DIM), lambda dd, bb, cc: (0, 0))
    out_dir = pl.BlockSpec((1, 1, C, R_DIM), lambda dd, bb, cc: (dd, bb, tmap(dd, cc), 0))
    state = pl.BlockSpec((1, 1, H_RWKV, N, N), lambda dd, bb, cc: (bb, dd, 0, 0, 0))
    in_specs = [
        pl.BlockSpec((1, C, 3 * R_DIM), lambda dd, bb, cc: (bb, tmap(dd, cc), 0)),
        pl.BlockSpec((1, 8, 3 * R_DIM), lambda dd, bb, cc: (bb, jnp.maximum(tmap(dd, cc) * hb - 1, 0), 0)),
        pl.BlockSpec((1, 8, 3 * R_DIM),
                     lambda dd, bb, cc: (bb, jnp.minimum((tmap(dd, cc) + 1) * hb, T // 8 - 1), 0)),
        pl.BlockSpec((1, C, 128), lambda dd, bb, cc: (bb, tmap(dd, cc), dd)),
        pl.BlockSpec((3, 3 * R_DIM), lambda dd, bb, cc: (0, 0)),
        vec, vec, vec, per_dir_vec, per_dir_vec, per_dir_mat, per_dir_mat, state]
    y, bonus, s_fin = pl.pallas_call(
        functools.partial(_scan_kernel, chunk=C, pairs=n_pairs, n_chunks=nC),
        grid=(N_DIR, B, nC),
        in_specs=in_specs,
        out_specs=[out_dir, out_dir, state],
        out_shape=[jax.ShapeDtypeStruct((N_DIR, B, T, R_DIM), F32),
                   jax.ShapeDtypeStruct((N_DIR, B, T, R_DIM), F32),
                   jax.ShapeDtypeStruct((B, N_DIR, H_RWKV, N, N), F32)],
        scratch_shapes=[pltpu.VMEM((n_pairs, N, PAIR), F32)],
        compiler_params=_params(("parallel", "parallel", "arbitrary")),
        name="rwkv_scan",
    )(z_rkv, z_rkv, z_rkv, lora, p['conv_rkv'], row3(p['k_k']), row3(p['k_a']), row3(p['r_k']),
      p['w0'].reshape(N_DIR, 1, R_DIM), p['a0'].reshape(N_DIR, 1, R_DIM), p['w_up_pad'], p['a_up_pad'],
      s0)
    return y, bonus, s_fin


def _post_kernel(y0_ref, y1_ref, b0_ref, b1_ref, gd_ref, gup_ref, lw_ref, lb_ref, o_ref):
    tm = o_ref.shape[0]
    g = jnp.dot(gd_ref[...].astype(BF16), gup_ref[...], preferred_element_type=F32)
    head0 = lax.broadcasted_iota(jnp.int32, (tm, PAIR), 1) < HEAD_RWKV

    def head_mean(x):
        s0 = jnp.sum(jnp.where(head0, x, 0.0), axis=-1, keepdims=True)
        s1 = jnp.sum(jnp.where(head0, 0.0, x), axis=-1, keepdims=True)
        return jnp.where(head0, s0, s1) * (1.0 / HEAD_RWKV)

    for p in range(R_DIM // PAIR):
        sl = slice(p * PAIR, (p + 1) * PAIR)
        ys = y0_ref[0, :, sl] + y1_ref[0, :, sl]
        dev = ys - head_mean(ys)
        yn = dev * lax.rsqrt(head_mean(dev * dev) + LNX_EPS)
        o = yn * lw_ref[:, sl] + lb_ref[:, sl] + (b0_ref[0, :, sl] + b1_ref[0, :, sl])
        o_ref[:, sl] = (o * g[:, sl]).astype(o_ref.dtype)


def _rwkv_post(y, bonus, lora, p, *, tm=256):
    _, M, Rd = y.shape
    tm = min(tm, M)
    assert M % tm == 0
    d0 = pl.BlockSpec((1, tm, Rd), lambda i: (0, i, 0))
    d1 = pl.BlockSpec((1, tm, Rd), lambda i: (1, i, 0))
    vec = pl.BlockSpec((1, Rd), lambda i: (0, 0))
    return pl.pallas_call(
        _post_kernel,
        grid=(M // tm,),
        in_specs=[d0, d1, d0, d1, pl.BlockSpec((tm, 128), lambda i: (i, 2)),
                  pl.BlockSpec((G_LORA, Rd), lambda i: (0, 0)), vec, vec],
        out_specs=pl.BlockSpec((tm, Rd), lambda i: (i, 0)),
        out_shape=jax.ShapeDtypeStruct((M, Rd), BF16),
        compiler_params=_params(("parallel",)),
        name="rwkv_post",
    )(y, y, bonus, bonus, lora, p['g_up_bf16'], p['lnx_w'].reshape(1, Rd), p['lnx_b'].reshape(1, Rd))


def _rope_tables(T, n_cache):
    rows = T // GRID_W
    row = jnp.repeat(jnp.arange(rows, dtype=F32), GRID_W)
    col = jnp.tile(jnp.arange(GRID_W, dtype=F32), rows)
    inv_freq = jnp.power(ROPE_THETA, -jnp.arange(AXIS_PAIRS, dtype=F32) / AXIS_PAIRS)
    ar, ac = row[:, None] * inv_freq, col[:, None] * inv_freq
    cos = jnp.concatenate([jnp.cos(ar), jnp.cos(ar), jnp.cos(ac), jnp.cos(ac)], axis=1)
    sin = jnp.concatenate([-jnp.sin(ar), jnp.sin(ar), -jnp.sin(ac), jnp.sin(ac)], axis=1)
    cos = jnp.concatenate([cos, jnp.ones((n_cache, ROPE_DIM), F32)], axis=0)
    sin = jnp.concatenate([sin, jnp.zeros((n_cache, ROPE_DIM), F32)], axis=0)
    return jnp.tile(cos, (1, 2)), jnp.tile(sin, (1, 2))


def _trunk_layer(x, mod, p, cache):
    B, T, _ = x.shape
    M = B * T
    nb = mod.shape[0]
    shift1, scale1, gate1, shift2, scale2, gate2 = [
        m.reshape(nb, 1, D_MODEL) for m in jnp.split(mod, 6, axis=-1)]
    xf = x.reshape(M, D_MODEL)
    latent = cache is not None
    proj = functools.partial(_norm_matmul, xf, p['norm1'], scale1, shift1, rows_per_batch=T)

    q_tm = 512
    extra, extra_specs = [p['q_gain']], [lambda tm: pl.BlockSpec((1, 2 * Q_PAIR), lambda i, j: (0, 0))]
    if latent:
        cos, sin = _rope_tables(T, cache[0].shape[1])
        n_t = T // min(q_tm, T)
        tbl = lambda tm: pl.BlockSpec((tm, 128), lambda i, j: (i % n_t, 0))
        extra += [cos[:T], sin[:T]]
        extra_specs += [tbl, tbl]
    q, = proj(p['w_q'], tm=q_tm, tn=2 * Q_PAIR, epilogue=functools.partial(_epi_q, rope=latent),
              outs=[(2 * Q_PAIR, Q_DIM, BF16)], extra=extra, extra_specs=extra_specs, name="proj_q")
    z_ckv, z_kr, lora = proj(p['w_small'], tm=1024, tn=SMALL_COLS, epilogue=_epi_small,
                             outs=[(KV_RANK, KV_RANK, F32), (ROPE_DIM, ROPE_DIM, F32), (384, 384, F32)],
                             name="proj_small")
    z_rkv, = proj(p['w_rkv'], tm=1024, tn=1024, epilogue=_epi_plain,
                  outs=[(1024, 3 * R_DIM, F32)], name="proj_rkv")
    gates, = proj(p['w_gate'], tm=1024, tn=1024, epilogue=_epi_sigmoid,
                  outs=[(1024, 2 * D_MODEL, BF16)], name="proj_gate")

    if latent:
        ckv_ctx, kr_ctx, s0 = cache
        S = T + ckv_ctx.shape[1]
        ckv_all = jnp.concatenate([z_ckv.reshape(B, T, KV_RANK), ckv_ctx], axis=1).reshape(B * S, KV_RANK)
        kr_all = jnp.concatenate([z_kr.reshape(B, T, ROPE_DIM), kr_ctx], axis=1).reshape(B * S, ROPE_DIM)
        k, v = _keys_values(ckv_all, kr_all, p, (cos, sin), tm=S)
    else:
        S = T
        s0 = jnp.zeros((B, N_DIR, H_RWKV, HEAD_RWKV, HEAD_RWKV), F32)
        k, v = _keys_values(z_ckv, z_kr, p, None, tm=1024)
    o_mla = _attention(q.reshape(B, T, Q_DIM), k, v, S, pairs=2 if latent else 4).reshape(M, MLA_WIDTH)

    y, bonus, s_final = _rwkv_scan(z_rkv.reshape(B, T, 3 * R_DIM), lora.reshape(B, T, 384), p, s0)
    o_rwkv = _rwkv_post(y.reshape(N_DIR, M, R_DIM), bonus.reshape(N_DIR, M, R_DIM), lora, p)

    merged = _branch_merge(o_mla, o_rwkv, gates, p['w_br_mla'], p['w_br_rwkv'])
    x1 = _out_proj(merged, p['w_out'], xf, gate1, T)
    x2 = _ffn(x1, p['norm2'], scale2, shift2, gate2, p['w_ff_in'], p['w_ff_out'], T)
    return (x2.reshape(B, T, D_MODEL),
            (z_ckv.reshape(B, T, KV_RANK), z_kr.reshape(B, T, ROPE_DIM), s_final))


def _split_w_in(w):
    offs = [sum(IN_SIZES[:i]) for i in range(len(IN_SIZES))]
    part = lambda i: w[:, offs[i]:offs[i] + IN_SIZES[i]]
    K = w.shape[0]
    wq = part(0).reshape(K, H_MLA // 2, 2, QK_HEAD)
    wq = jnp.concatenate([wq[..., :NOPE_DIM].reshape(K, H_MLA // 2, 2 * NOPE_DIM),
                          wq[..., NOPE_DIM:].reshape(K, H_MLA // 2, 2 * ROPE_DIM)], axis=-1)
    wd, ad = part(4), part(5)
    small = jnp.concatenate([part(1), part(2), jnp.zeros((K, 128 - ROPE_DIM), w.dtype),
                             wd[:, :W_LORA], ad[:, :A_LORA], wd[:, W_LORA:], ad[:, A_LORA:],
                             part(6)], axis=1)
    return (wq.reshape(K, Q_DIM).astype(BF16), small.astype(BF16), part(3).astype(BF16),
            part(7).astype(BF16))


def _layer_params(l, w_in, q_norm, w_kv_up, w_br_mla, w_br_rwkv, w_out, w_ff_in, w_ff_out, **small):
    zeros = jnp.zeros((N_DIR, W_LORA, R_DIM), F32)
    w_q, w_small, w_rkv, w_gate = _split_w_in(w_in[l])
    qg = q_norm[l] * (QK_HEAD ** -0.5)
    q_gain = jnp.concatenate([qg[:NOPE_DIM], qg[:NOPE_DIM], qg[NOPE_DIM:], qg[NOPE_DIM:]])
    p = {name: val[l] for name, val in small.items()}
    p.update({
        'w_q': w_q, 'w_small': w_small, 'w_rkv': w_rkv, 'w_gate': w_gate,
        'q_gain': jnp.tile(q_gain, 2).reshape(1, 2 * Q_PAIR),
        'w_kv_up': w_kv_up[l].astype(BF16),
        'w_up_pad': jnp.concatenate([small['w_up'][l], zeros], axis=1).astype(BF16),
        'a_up_pad': jnp.concatenate([zeros, small['a_up'][l]], axis=1).astype(BF16),
        'g_up_bf16': small['g_up'][l].astype(BF16),
        'w_br_mla': w_br_mla[l].astype(BF16), 'w_br_rwkv': w_br_rwkv[l].astype(BF16),
        'w_out': w_out[l].astype(BF16),
        'w_ff_in': w_ff_in[l].astype(BF16), 'w_ff_out': w_ff_out[l].astype(BF16),
    })
    return p


def kernel(x_prompt, x_sample, cache_mla_ckv, cache_mla_kr, state_rwkv, c, c_ctx,
           norm1, w_ada, b_ada, w_in, q_norm, kv_norm, w_kv_up, k_norm, conv_rkv,
           k_k, k_a, r_k, w0, w_up, a0, a_up, g_up, lnx_w, lnx_b,
           w_br_mla, w_br_rwkv, w_out, norm2, w_ff_in, w_ff_out):
    x_p, x_s = x_prompt, x_sample
    n_lat = c.shape[0]
    ckv_list, kr_list, st_list = [], [], []
    for l in range(DEPTH):
        p = _layer_params(l, w_in, q_norm, w_kv_up, w_br_mla, w_br_rwkv, w_out, w_ff_in, w_ff_out,
                          norm1=norm1, kv_norm=kv_norm, k_norm=k_norm, conv_rkv=conv_rkv, k_k=k_k,
                          k_a=k_a, r_k=r_k, w0=w0, w_up=w_up, a0=a0, a_up=a_up, g_up=g_up,
                          lnx_w=lnx_w, lnx_b=lnx_b, norm2=norm2)
        cond = jnp.concatenate([c, c_ctx[None]], axis=0)
        cond = jnp.pad(jax.nn.silu(cond), ((0, (-cond.shape[0]) % 16), (0, 0)))
        mod = _matmul(cond, w_ada[l], tm=16, tn=1024)[:n_lat + 1] + b_ada[l]
        x_p, (ckv_l, kr_l, st_l) = _trunk_layer(x_p, mod[n_lat:], p, None)
        ckv_list.append(ckv_l)
        kr_list.append(kr_l)
        st_list.append(st_l)
        x_s, _ = _trunk_layer(x_s, mod[:n_lat], p,
                              (cache_mla_ckv[:, l], cache_mla_kr[:, l], state_rwkv[:, l]))
    return (x_p, x_s, jnp.stack(ckv_list, axis=1), jnp.stack(kr_list, axis=1),
            jnp.stack(st_list, axis=1))
```

```python
import functools

import jax
import jax.numpy as jnp
from jax import lax
from jax.experimental import pallas as pl
from jax.experimental.pallas import tpu as pltpu

D_MODEL = 2048
DEPTH = 1
GRID_W = 64
H_MLA = 16
NOPE_DIM = 128
ROPE_DIM = 64
QK_HEAD = NOPE_DIM + ROPE_DIM
V_HEAD = 128
KV_RANK = 512
ROPE_THETA = 10000.0
AXIS_DIM = ROPE_DIM // 2
AXIS_PAIRS = AXIS_DIM // 2
MLA_WIDTH = H_MLA * V_HEAD
Q_DIM = H_MLA * QK_HEAD
HEAD_RWKV = 64
H_RWKV = D_MODEL // HEAD_RWKV
R_DIM = H_RWKV * HEAD_RWKV
W_LORA = 64
A_LORA = 64
G_LORA = 128
N_DIR = 2
LNX_EPS = 64e-5
D_FF = 4 * D_MODEL
EPS = 1e-6
IN_SIZES = (Q_DIM, KV_RANK, ROPE_DIM, 3 * R_DIM, N_DIR * W_LORA, N_DIR * A_LORA, G_LORA, 2 * D_MODEL)
IN_DIM = sum(IN_SIZES)

F32 = jnp.float32
BF16 = jnp.bfloat16
VMEM_LIMIT = 56 * 1024 * 1024
SCAN_CHUNK = 64
SCAN_PAIRS = 16


def _params(sem):
    return pltpu.CompilerParams(dimension_semantics=sem, vmem_limit_bytes=VMEM_LIMIT)


def _mm_kernel(a_ref, b_ref, o_ref):
    o_ref[...] = jnp.dot(a_ref[...].astype(BF16), b_ref[...],
                         preferred_element_type=F32).astype(o_ref.dtype)


def _matmul(a, b, *, tm=512, tn=512, out_dtype=F32):
    M, K = a.shape
    N = b.shape[1]
    tm = min(tm, M)
    tn = min(tn, N)
    assert M % tm == 0 and N % tn == 0, (M, N, tm, tn)
    return pl.pallas_call(
        _mm_kernel,
        grid=(M // tm, N // tn),
        in_specs=[pl.BlockSpec((tm, K), lambda i, j: (i, 0)),
                  pl.BlockSpec((K, tn), lambda i, j: (0, j))],
        out_specs=pl.BlockSpec((tm, tn), lambda i, j: (i, j)),
        out_shape=jax.ShapeDtypeStruct((M, N), out_dtype),
        compiler_params=_params(("parallel", "arbitrary")),
        name="matmul",
    )(a, b.astype(BF16))


def _modulated_norm(x, g, scale, shift):
    ms = jnp.mean(x * x, axis=-1, keepdims=True)
    return (x * lax.rsqrt(ms + EPS) * g) * (1.0 + scale) + shift


def _mod_index(nb, rows_per_batch, tm):
    if nb == 1:
        return lambda i, j: (0, 0, 0)
    assert rows_per_batch % tm == 0
    return lambda i, j: (i * tm // rows_per_batch, 0, 0)


def _row_tile(tm, M, nb, rows_per_batch):
    tm = min(tm, M, rows_per_batch) if nb > 1 else min(tm, M)
    assert M % tm == 0
    return tm


def _sigmoid(x):
    return 1.0 / (1.0 + jnp.exp(-x))


def _swap16(x):
    lane = lax.broadcasted_iota(jnp.int32, x.shape, 1)
    return jnp.where((lane & AXIS_PAIRS) == 0,
                     pltpu.roll(x, 128 - AXIS_PAIRS, 1), pltpu.roll(x, AXIS_PAIRS, 1))


def _epi_plain(acc, extra, outs, rows):
    outs[0][rows, :] = acc.astype(outs[0].dtype)


def _epi_sigmoid(acc, extra, outs, rows):
    outs[0][rows, :] = _sigmoid(acc).astype(outs[0].dtype)


SMALL_LORA = KV_RANK + 128
SMALL_COLS = SMALL_LORA + 3 * 128


def _epi_small(acc, extra, outs, rows):
    ckv_ref, kr_ref, lora_ref = outs
    ckv_ref[rows, :] = acc[:, :KV_RANK]
    kr_ref[rows, :] = acc[:, KV_RANK:KV_RANK + ROPE_DIM]
    o = SMALL_LORA
    wa = acc[:, o:o + 256]
    lane = lax.broadcasted_iota(jnp.int32, wa.shape, 1)
    wa = jnp.where((lane & W_LORA) == 0, jnp.tanh(wa), wa)
    lora_ref[rows, :] = jnp.concatenate([wa, _sigmoid(acc[:, o + 256:o + 384])], axis=1)


Q_PAIR = 2 * QK_HEAD


def _epi_q(acc, extra, outs, rows, *, rope):
    g_ref = extra[0]
    tm, tn = acc.shape
    lo = lax.broadcasted_iota(jnp.int32, (tm, 128), 1) < ROPE_DIM
    parts = []
    for p in range(tn // Q_PAIR):
        z = acc[:, p * Q_PAIR:(p + 1) * Q_PAIR]
        g = g_ref[:, p * Q_PAIR:(p + 1) * Q_PAIR]
        n0, n1, rp = z[:, :128], z[:, 128:256], z[:, 256:]
        rp2 = rp * rp
        s0 = (jnp.sum(n0 * n0, axis=-1, keepdims=True)
              + jnp.sum(jnp.where(lo, rp2, 0.0), axis=-1, keepdims=True))
        s1 = (jnp.sum(n1 * n1, axis=-1, keepdims=True)
              + jnp.sum(jnp.where(lo, 0.0, rp2), axis=-1, keepdims=True))
        r0 = lax.rsqrt(s0 * (1.0 / QK_HEAD) + EPS)
        r1 = lax.rsqrt(s1 * (1.0 / QK_HEAD) + EPS)
        qr = rp * jnp.where(lo, r0, r1) * g[:, 256:]
        if rope:
            qr = qr * extra[1][rows, :] + _swap16(qr) * extra[2][rows, :]
        parts += [n0 * r0 * g[:, :128], n1 * r1 * g[:, 128:256], qr]
    outs[0][rows, :] = jnp.concatenate(parts, axis=1).astype(outs[0].dtype)


def _norm_mm_kernel(*refs, n_extra, n_out, epilogue, row_split):
    x_ref, g_ref, sc_ref, sh_ref, w_ref = refs[:5]
    extra = refs[5:5 + n_extra]
    outs = refs[5 + n_extra:5 + n_extra + n_out]
    h_scr = refs[-1]

    @pl.when(pl.program_id(1) == 0)
    def _():
        h = _modulated_norm(x_ref[...], g_ref[...], sc_ref[0], sh_ref[0])
        h_scr[...] = h.astype(BF16)

    step = h_scr.shape[0] // row_split
    for r in range(row_split):
        rows = pl.ds(r * step, step)
        epilogue(jnp.dot(h_scr[rows, :], w_ref[...], preferred_element_type=F32), extra, outs, rows)


def _norm_matmul(x, g, scale, shift, w, rows_per_batch, *, tm, tn, epilogue, outs,
                 extra=(), extra_specs=(), row_split=2, name):
    M, K = x.shape
    N = w.shape[1]
    nb = scale.shape[0]
    tm = _row_tile(tm, M, nb, rows_per_batch)
    assert N % tn == 0 and tm % (16 * row_split) == 0
    midx = _mod_index(nb, rows_per_batch, tm)
    res = pl.pallas_call(
        functools.partial(_norm_mm_kernel, n_extra=len(extra), n_out=len(outs), epilogue=epilogue,
                          row_split=row_split),
        grid=(M // tm, N // tn),
        in_specs=[pl.BlockSpec((tm, K), lambda i, j: (i, 0)),
                  pl.BlockSpec((1, K), lambda i, j: (0, 0)),
                  pl.BlockSpec((1, 1, K), midx),
                  pl.BlockSpec((1, 1, K), midx),
                  pl.BlockSpec((K, tn), lambda i, j: (0, j))] + [s(tm) for s in extra_specs],
        out_specs=[pl.BlockSpec((tm, c), lambda i, j: (i, j)) for c, _, _ in outs],
        out_shape=[jax.ShapeDtypeStruct((M, n), dt) for _, n, dt in outs],
        scratch_shapes=[pltpu.VMEM((tm, K), BF16)],
        compiler_params=_params(("parallel", "arbitrary")),
        name=name,
    )(x, g.reshape(1, K), scale, shift, w, *extra)
    return res


K_WIDTH = 2 * NOPE_DIM
KV_ROW_SPLIT = 4


def _kv_kernel(*refs, rope):
    if rope:
        (ckv_ref, kr_ref, gkv_ref, gk_ref, cos_ref, sin_ref, w_ref, k_ref, v_ref,
         a_scr, kr_scr, ss_scr) = refs
    else:
        ckv_ref, kr_ref, gkv_ref, gk_ref, w_ref, k_ref, v_ref, a_scr, kr_scr, ss_scr = refs
    h = pl.program_id(1)

    @pl.when(h == 0)
    def _():
        c = ckv_ref[...]
        a = c * lax.rsqrt(jnp.mean(c * c, axis=-1, keepdims=True) + EPS) * gkv_ref[...]
        a_scr[...] = a.astype(BF16)
        kr = kr_ref[...]
        ss_scr[...] = jnp.broadcast_to(0.5 * jnp.sum(kr * kr, axis=-1, keepdims=True), ss_scr.shape)
        krg = kr * gk_ref[:, NOPE_DIM:]
        if rope:
            krg = krg * cos_ref[...] + _swap16(krg) * sin_ref[...]
        kr_scr[...] = krg

    step = a_scr.shape[0] // KV_ROW_SPLIT
    for r in range(KV_ROW_SPLIT):
        rows = pl.ds(r * step, step)
        acc = jnp.dot(a_scr[rows, :], w_ref[...], preferred_element_type=F32)
        kn = acc[:, :NOPE_DIM]
        ssq = jnp.sum(kn * kn, axis=-1, keepdims=True) + ss_scr[rows, :1]
        rstd = lax.rsqrt(ssq * (1.0 / QK_HEAD) + EPS)
        lane = lax.broadcasted_iota(jnp.int32, kn.shape, 1)
        mine = (lane >> 6) == (h & 1)
        krh = jnp.where(mine, kr_scr[rows, :] * rstd, 0.0)
        k_ref[0, rows, :] = jnp.concatenate([kn * rstd * gk_ref[:, :NOPE_DIM], krh], axis=1).astype(BF16)
        v_ref[0, rows, :] = acc[:, NOPE_DIM:].astype(BF16)


def _keys_values(ckv, kr, p, tables, *, tm):
    Mk = ckv.shape[0]
    tm = min(tm, Mk)
    assert Mk % tm == 0 and tm % (16 * KV_ROW_SPLIT) == 0
    rope = tables is not None
    kr2 = jnp.concatenate([kr, kr], axis=-1)
    gk = jnp.concatenate([p['k_norm'], p['k_norm'][NOPE_DIM:]]).reshape(1, K_WIDTH)
    row = lambda i, h: (i, 0)
    fixed = lambda i, h: (0, 0)
    in_specs = [pl.BlockSpec((tm, KV_RANK), row), pl.BlockSpec((tm, 128), row),
                pl.BlockSpec((1, KV_RANK), fixed), pl.BlockSpec((1, K_WIDTH), fixed)]
    args = [ckv, kr2, p['kv_norm'].reshape(1, KV_RANK), gk]
    if rope:
        in_specs += [pl.BlockSpec((tm, 128), fixed), pl.BlockSpec((tm, 128), fixed)]
        args += list(tables)
    in_specs.append(pl.BlockSpec((KV_RANK, NOPE_DIM + V_HEAD), lambda i, h: (0, h)))
    args.append(p['w_kv_up'])
    return pl.pallas_call(
        functools.partial(_kv_kernel, rope=rope),
        grid=(Mk // tm, H_MLA),
        in_specs=in_specs,
        out_specs=[pl.BlockSpec((1, tm, K_WIDTH), lambda i, h: (h, i, 0)),
                   pl.BlockSpec((1, tm, V_HEAD), lambda i, h: (h, i, 0))],
        out_shape=[jax.ShapeDtypeStruct((H_MLA, Mk, K_WIDTH), BF16),
                   jax.ShapeDtypeStruct((H_MLA, Mk, V_HEAD), BF16)],
        scratch_shapes=[pltpu.VMEM((tm, KV_RANK), BF16), pltpu.VMEM((tm, 128), F32),
                        pltpu.VMEM((tm, 128), F32)],
        compiler_params=_params(("parallel", "arbitrary")),
        name="keys_values",
    )(*args)


def _attn_kernel(q_ref, k_ref, v_ref, o_ref, *, pairs):
    for pr in range(pairs):
        q = q_ref[0, :, pr * Q_PAIR:(pr + 1) * Q_PAIR]
        qr = q[:, 2 * NOPE_DIM:]
        for h in range(2):
            hh = 2 * pr + h
            qh = jnp.concatenate([q[:, h * NOPE_DIM:(h + 1) * NOPE_DIM], qr], axis=1)
            s = lax.dot_general(qh, k_ref[hh], (((1,), (1,)), ((), ())), preferred_element_type=F32)
            m = jnp.max(s, axis=-1, keepdims=True)
            p = jnp.exp(s - m)
            l = jnp.sum(p, axis=-1, keepdims=True)
            o = jnp.dot(p.astype(BF16), v_ref[hh], preferred_element_type=F32) / l
            o_ref[0, :, hh * V_HEAD:(hh + 1) * V_HEAD] = o.astype(o_ref.dtype)


def _attention(q, k, v, S, *, tq=256, pairs):
    B, T, _ = q.shape
    tq = min(tq, T)
    hp = 2 * pairs
    return pl.pallas_call(
        functools.partial(_attn_kernel, pairs=pairs),
        grid=(B, H_MLA // hp, T // tq),
        in_specs=[pl.BlockSpec((1, tq, pairs * Q_PAIR), lambda b, p, i: (b, i, p)),
                  pl.BlockSpec((hp, S, K_WIDTH), lambda b, p, i: (p, b, 0)),
                  pl.BlockSpec((hp, S, V_HEAD), lambda b, p, i: (p, b, 0))],
        out_specs=pl.BlockSpec((1, tq, hp * V_HEAD), lambda b, p, i: (b, i, p)),
        out_shape=jax.ShapeDtypeStruct((B, T, MLA_WIDTH), BF16),
        compiler_params=_params(("parallel", "parallel", "arbitrary")),
        name="attention",
    )(q, k, v)


def _branch_kernel(a_ref, b_ref, ga_ref, gb_ref, wa_ref, wb_ref, o_ref):
    ya = jnp.dot(a_ref[...], wa_ref[...], preferred_element_type=F32)
    yb = jnp.dot(b_ref[...], wb_ref[...], preferred_element_type=F32)
    o_ref[...] = (ga_ref[...].astype(F32) * ya + gb_ref[...].astype(F32) * yb).astype(o_ref.dtype)


def _branch_merge(a, b, gates, wa, wb, *, tm=1024, tn=512):
    M, K = a.shape
    N = wa.shape[1]
    tm = min(tm, M)
    nj = N // tn
    return pl.pallas_call(
        _branch_kernel,
        grid=(M // tm, nj),
        in_specs=[pl.BlockSpec((tm, K), lambda i, j: (i, 0)),
                  pl.BlockSpec((tm, K), lambda i, j: (i, 0)),
                  pl.BlockSpec((tm, tn), lambda i, j: (i, j)),
                  pl.BlockSpec((tm, tn), lambda i, j: (i, j + nj)),
                  pl.BlockSpec((K, tn), lambda i, j: (0, j)),
                  pl.BlockSpec((K, tn), lambda i, j: (0, j))],
        out_specs=pl.BlockSpec((tm, tn), lambda i, j: (i, j)),
        out_shape=jax.ShapeDtypeStruct((M, N), BF16),
        compiler_params=_params(("parallel", "arbitrary")),
        name="branch_merge",
    )(a, b, gates, gates, wa, wb)


def _resid_kernel(m_ref, w_ref, x_ref, gt_ref, o_ref):
    o_ref[...] = x_ref[...] + gt_ref[0] * jnp.dot(m_ref[...], w_ref[...],
                                                  preferred_element_type=F32)


def _out_proj(m, w, x, gate, rows_per_batch, *, tm=1024, tn=512):
    M, K = m.shape
    N = w.shape[1]
    nb = gate.shape[0]
    tm = _row_tile(tm, M, nb, rows_per_batch)
    if nb == 1:
        gidx = lambda i, j: (0, 0, j)
    else:
        gidx = lambda i, j: (i * tm // rows_per_batch, 0, j)
    return pl.pallas_call(
        _resid_kernel,
        grid=(M // tm, N // tn),
        in_specs=[pl.BlockSpec((tm, K), lambda i, j: (i, 0)),
                  pl.BlockSpec((K, tn), lambda i, j: (0, j)),
                  pl.BlockSpec((tm, tn), lambda i, j: (i, j)),
                  pl.BlockSpec((1, 1, tn), gidx)],
        out_specs=pl.BlockSpec((tm, tn), lambda i, j: (i, j)),
        out_shape=jax.ShapeDtypeStruct((M, N), F32),
        compiler_params=_params(("parallel", "arbitrary")),
        name="out_proj",
    )(m, w, x, gate)


def _ffn_kernel(x_ref, g_ref, sc_ref, sh_ref, gt_ref, w1_ref, w2_ref, o_ref, h_scr, acc_scr):
    f = pl.program_id(1)

    @pl.when(f == 0)
    def _():
        h = _modulated_norm(x_ref[...], g_ref[...], sc_ref[0], sh_ref[0])
        h_scr[...] = h.astype(BF16)
        acc_scr[...] = jnp.zeros_like(acc_scr)

    u = jnp.dot(h_scr[...], w1_ref[...], preferred_element_type=F32)
    u = jnp.square(jnp.maximum(u, 0.0))
    acc_scr[...] += jnp.dot(u.astype(BF16), w2_ref[...], preferred_element_type=F32)

    @pl.when(f == pl.num_programs(1) - 1)
    def _():
        o_ref[...] = x_ref[...] + gt_ref[0] * acc_scr[...]


def _ffn(x, g, scale, shift, gate, w1, w2, rows_per_batch, *, tm=512, tf=512):
    M, K = x.shape
    F = w1.shape[1]
    nb = scale.shape[0]
    tm = _row_tile(tm, M, nb, rows_per_batch)
    assert F % tf == 0
    midx = _mod_index(nb, rows_per_batch, tm)
    return pl.pallas_call(
        _ffn_kernel,
        grid=(M // tm, F // tf),
        in_specs=[pl.BlockSpec((tm, K), lambda i, j: (i, 0)),
                  pl.BlockSpec((1, K), lambda i, j: (0, 0)),
                  pl.BlockSpec((1, 1, K), midx),
                  pl.BlockSpec((1, 1, K), midx),
                  pl.BlockSpec((1, 1, K), midx),
                  pl.BlockSpec((K, tf), lambda i, j: (0, j)),
                  pl.BlockSpec((tf, K), lambda i, j: (j, 0))],
        out_specs=pl.BlockSpec((tm, K), lambda i, j: (i, 0)),
        out_shape=jax.ShapeDtypeStruct((M, K), F32),
        scratch_shapes=[pltpu.VMEM((tm, K), BF16), pltpu.VMEM((tm, K), F32)],
        compiler_params=_params(("parallel", "arbitrary")),
        name="ffn",
    )(x, g.reshape(1, K), scale, shift, gate, w1, w2)


def _bdot(a, b, dims=((1,), (0,))):
    return lax.dot_general(a.astype(BF16), b.astype(BF16), (dims, ((), ())),
                           preferred_element_type=F32)


_NT = ((1,), (1,))
_TN = ((0,), (0,))
PAIR = 2 * HEAD_RWKV


def _scan_kernel(z0_ref, zp0_ref, zn0_ref, lo0_ref, z1_ref, zp1_ref, zn1_ref, lo1_ref,
                 cw_ref, kkg_ref, ka_ref, rk_ref, w0_ref, a0_ref, wup_ref, aup_ref, s0_ref,
                 y0_ref, y1_ref, bon0_ref, bon1_ref, sf_ref, s_scr, *, chunk, pairs, n_chunks):
    c = pl.program_id(1)
    C = chunk
    N = HEAD_RWKV

    @pl.when(c == 0)
    def _():
        for d in range(N_DIR):
            for p in range(pairs):
                s_scr[d, p] = jnp.concatenate([s0_ref[0, d, 2 * p], s0_ref[0, d, 2 * p + 1]], axis=1)

    row = lax.broadcasted_iota(jnp.int32, (C, PAIR), 0)
    lane = lax.broadcasted_iota(jnp.int32, (C, PAIR), 1)
    col = lane & (N - 1)
    eye = row == col
    head0 = lane < N
    ti = lax.broadcasted_iota(jnp.int32, (C, C), 0)
    tj = lax.broadcasted_iota(jnp.int32, (C, C), 1)

    def bd(x):
        return jnp.concatenate([jnp.where(head0, x, 0.0), jnp.where(head0, 0.0, x)],
                               axis=0).astype(BF16)

    def diag_blocks(x):
        return jnp.where(head0, x[:N], x[N:])

    def head_sum(x):
        s0 = jnp.sum(jnp.where(head0, x, 0.0), axis=-1, keepdims=True)
        s1 = jnp.sum(jnp.where(head0, 0.0, x), axis=-1, keepdims=True)
        return jnp.where(head0, s0, s1)

    def direction(d, z_ref, zp_ref, zn_ref, lo_ref, y_ref, bon_ref):
        tt = c if d == 0 else n_chunks - 1 - c
        first = tt == 0
        last = tt == n_chunks - 1
        if d == 0:
            incl, strict, tri = row >= col, row > col, ti >= tj
        else:
            incl, strict, tri = row <= col, row < col, ti <= tj
        tri = jnp.where(tri, 1.0, 0.0).astype(BF16)
        lora = lo_ref[0].astype(BF16)

        def conv(off, sl):
            zs = slice(off + sl.start, off + sl.stop)
            zc = z_ref[0, :, zs]
            before = jnp.where(first, 0.0, zp_ref[0, 7:8, zs])
            after = jnp.where(last, 0.0, zn_ref[0, 0:1, zs])
            zm = jnp.where(row == 0, before, pltpu.roll(zc, 1, 0))
            zq = jnp.where(row == C - 1, after, pltpu.roll(zc, C - 1, 0))
            return zm * cw_ref[0:1, zs] + zc * cw_ref[1:2, zs] + zq * cw_ref[2:3, zs]

        def s_lora(st, sl, p):
            u = w0_ref[d, :, sl] + jnp.dot(lora, wup_ref[d, :, sl], preferred_element_type=F32)
            softplus = jnp.maximum(-u, 0.0) + jnp.log(1.0 + jnp.exp(-jnp.abs(u)))
            st['ld'] = -jnp.exp(-softplus - 0.5)
            st['a'] = _sigmoid(a0_ref[d, :, sl] + jnp.dot(lora, aup_ref[d, :, sl],
                                                         preferred_element_type=F32))

        def s_cum(st, sl, p):
            ld = st['ld']
            ld_hi = ld.astype(BF16)
            rem = ld - ld_hi.astype(F32)
            ld_mid = rem.astype(BF16)
            ld_lo = (rem - ld_mid.astype(F32)).astype(BF16)
            st['cum'] = (jnp.dot(tri, ld_hi, preferred_element_type=F32)
                         + jnp.dot(tri, ld_mid, preferred_element_type=F32)
                         + jnp.dot(tri, ld_lo, preferred_element_type=F32))
            st['tot'] = jnp.sum(ld, axis=0, keepdims=True)

        def s_conv(st, sl, p):
            st['R'] = conv(0, sl)
            st['Kraw'] = conv(R_DIM, sl)
            st['V'] = conv(2 * R_DIM, sl)

        def s_keys(st, sl, p):
            kx = st['Kraw'] * kkg_ref[:, sl]
            kk = kx * lax.rsqrt(head_sum(kx * kx) + 1e-12)
            a = st.pop('a')
            st['K'] = st.pop('Kraw') * (1.0 + (a - 1.0) * ka_ref[:, sl])
            st['Bv'] = kk * a
            st['kk'] = kk
            bon_ref[0, :, sl] = head_sum(st['R'] * st['K'] * rk_ref[:, sl]) * st['V']

        def s_decay(st, sl, p):
            cum, ld, tot = st.pop('cum'), st.pop('ld'), st['tot']
            e_neg = jnp.exp(-cum)
            e_end = jnp.exp(tot - cum)
            Bv, K = st.pop('Bv'), st.pop('K')
            st['At'] = -st.pop('kk') * jnp.exp(cum - ld)
            st['Rt'] = st.pop('R') * jnp.exp(cum)
            st['b_d'] = Bv * e_end
            st['k_d'] = K * e_end
            st['Bt'] = Bv * e_neg
            st['Kt'] = K * e_neg

        def s_scores(st, sl, p):
            st['bdv'] = bd(st['V'])
            st['sc'] = _bdot(jnp.concatenate([st['At'], st['Rt']], axis=0),
                             jnp.concatenate([bd(st.pop('Bt')), bd(st.pop('Kt'))], axis=0),
                             _NT)

        def s_masks(st, sl, p):
            sc = st.pop('sc')
            L = jnp.where(strict, sc[:C, :PAIR], 0.0)
            st['p_rb'] = jnp.where(incl, sc[C:, :PAIR], 0.0)
            st['p_rk'] = jnp.where(incl, sc[C:, PAIR:], 0.0)
            st['akv'] = _bdot(jnp.where(strict, sc[:C, PAIR:], 0.0), st['bdv'])
            st['x'] = jnp.where(eye, 1.0, 0.0) + L
            st['lp'] = _bdot(L, bd(L))

        def s_double(st, sl, p):
            o = _bdot(st['lp'], jnp.concatenate([bd(st['x']), bd(st['lp'])], axis=1))
            st['x'] = st['x'] + o[:, :PAIR]
            st['lp'] = o[:, PAIR:]

        def s_double_last(st, sl, p):
            st['x'] = st['x'] + _bdot(st.pop('lp'), bd(st['x']))

        def s_solve(st, sl, p):
            z = _bdot(st.pop('x'), jnp.concatenate([bd(st.pop('At')), bd(st.pop('akv'))], axis=1))
            st['a_p'], st['u_loc'] = z[:, :PAIR], z[:, PAIR:]

        def s_local(st, sl, p):
            a_p, u_loc = st.pop('a_p'), st.pop('u_loc')
            f = _bdot(st.pop('p_rb'), jnp.concatenate([bd(a_p), bd(u_loc)], axis=1))
            st['r_p'] = st.pop('Rt') + f[:, :PAIR]
            st['y_loc'] = f[:, PAIR:] + _bdot(st.pop('p_rk'), st.pop('bdv'))
            st['g'] = diag_blocks(_bdot(a_p, st['b_d'], _TN))
            st['s_loc'] = diag_blocks(_bdot(jnp.concatenate([u_loc, st.pop('V')], axis=0),
                                            jnp.concatenate([st.pop('b_d'), st.pop('k_d')], axis=0),
                                            _TN))

        def s_state(st, sl, p):
            S = s_scr[d, p]
            y_ref[0, :, sl] = _bdot(st.pop('r_p'), bd(S), _NT) + st.pop('y_loc')
            s_scr[d, p] = S * jnp.exp(st.pop('tot')) + _bdot(S, bd(st.pop('g'))) + st.pop('s_loc')

        prep = [s_lora, s_cum, s_conv, s_keys, s_decay]
        matrix = [s_scores, s_masks]
        n = 2
        while n < C:
            matrix.append(s_double if 2 * n < C else s_double_last)
            n *= 2
        matrix += [s_solve, s_local, s_state]
        return prep, matrix

    sls = [slice(p * PAIR, (p + 1) * PAIR) for p in range(pairs)]
    prep0, mat0 = direction(0, z0_ref, zp0_ref, zn0_ref, lo0_ref, y0_ref, bon0_ref)
    prep1, mat1 = direction(1, z1_ref, zp1_ref, zn1_ref, lo1_ref, y1_ref, bon1_ref)
    st0 = [dict() for _ in range(pairs)]
    st1 = [dict() for _ in range(pairs)]

    for fn in prep0:
        for p in range(pairs):
            fn(st0[p], sls[p], p)
    prep1_calls = [(fn, p) for fn in prep1 for p in range(pairs)]
    per = -(-len(prep1_calls) // len(mat0))
    for k, fn in enumerate(mat0):
        for p in range(pairs):
            fn(st0[p], sls[p], p)
        for pf, p in prep1_calls[k * per:(k + 1) * per]:
            pf(st1[p], sls[p], p)
    for fn in mat1:
        for p in range(pairs):
            fn(st1[p], sls[p], p)

    @pl.when(c == n_chunks - 1)
    def _():
        for d in range(N_DIR):
            for p in range(pairs):
                sf_ref[0, d, 2 * p] = s_scr[d, p, :, :N]
                sf_ref[0, d, 2 * p + 1] = s_scr[d, p, :, N:]


def _rwkv_scan(z_rkv, lora, p, s0):
    B, T, _ = z_rkv.shape
    C = SCAN_CHUNK
    nC = T // C
    n_pairs = H_RWKV // 2
    N = HEAD_RWKV
    hb = C // 8

    fwd = lambda cc: cc
    bwd = lambda cc: nC - 1 - cc

    def chunk_specs(tm, dd):
        return [
            pl.BlockSpec((1, C, 3 * R_DIM), lambda bb, cc: (bb, tm(cc), 0)),
            pl.BlockSpec((1, 8, 3 * R_DIM), lambda bb, cc: (bb, jnp.maximum(tm(cc) * hb - 1, 0), 0)),
            pl.BlockSpec((1, 8, 3 * R_DIM),
                         lambda bb, cc: (bb, jnp.minimum((tm(cc) + 1) * hb, T // 8 - 1), 0)),
            pl.BlockSpec((1, C, 128), lambda bb, cc: (bb, tm(cc), dd))]

    row3 = lambda a: a.reshape(1, -1)
    whole = lambda shape: pl.BlockSpec(shape, lambda bb, cc: (0,) * len(shape))
    vec = whole((1, R_DIM))
    out0 = pl.BlockSpec((1, C, R_DIM), lambda bb, cc: (bb, fwd(cc), 0))
    out1 = pl.BlockSpec((1, C, R_DIM), lambda bb, cc: (bb, bwd(cc), 0))
    state = pl.BlockSpec((1, N_DIR, H_RWKV, N, N), lambda bb, cc: (bb, 0, 0, 0, 0))
    in_specs = (chunk_specs(fwd, 0) + chunk_specs(bwd, 1)
                + [whole((3, 3 * R_DIM)), vec, vec, vec, whole((N_DIR, 1, R_DIM)), whole((N_DIR, 1, R_DIM)),
                   whole((N_DIR, 128, R_DIM)), whole((N_DIR, 128, R_DIM)), state])
    ydir = jax.ShapeDtypeStruct((B, T, R_DIM), F32)
    y0, y1, b0, b1, s_fin = pl.pallas_call(
        functools.partial(_scan_kernel, chunk=C, pairs=n_pairs, n_chunks=nC),
        grid=(B, nC),
        in_specs=in_specs,
        out_specs=[out0, out1, out0, out1, state],
        out_shape=[ydir, ydir, ydir, ydir, jax.ShapeDtypeStruct((B, N_DIR, H_RWKV, N, N), F32)],
        scratch_shapes=[pltpu.VMEM((N_DIR, n_pairs, N, PAIR), F32)],
        compiler_params=_params(("parallel", "arbitrary")),
        name="rwkv_scan",
    )(z_rkv, z_rkv, z_rkv, lora, z_rkv, z_rkv, z_rkv, lora, p['conv_rkv'], row3(p['k_k']),
      row3(p['k_a']), row3(p['r_k']), p['w0'].reshape(N_DIR, 1, R_DIM), p['a0'].reshape(N_DIR, 1, R_DIM),
      p['w_up_pad'], p['a_up_pad'], s0)
    return (y0, y1, b0, b1), s_fin


def _post_kernel(y0_ref, y1_ref, b0_ref, b1_ref, gd_ref, gup_ref, lw_ref, lb_ref, o_ref):
    tm = o_ref.shape[0]
    g = jnp.dot(gd_ref[...].astype(BF16), gup_ref[...], preferred_element_type=F32)
    head0 = lax.broadcasted_iota(jnp.int32, (tm, PAIR), 1) < HEAD_RWKV

    def head_mean(x):
        s0 = jnp.sum(jnp.where(head0, x, 0.0), axis=-1, keepdims=True)
        s1 = jnp.sum(jnp.where(head0, 0.0, x), axis=-1, keepdims=True)
        return jnp.where(head0, s0, s1) * (1.0 / HEAD_RWKV)

    for p in range(R_DIM // PAIR):
        sl = slice(p * PAIR, (p + 1) * PAIR)
        ys = y0_ref[:, sl] + y1_ref[:, sl]
        dev = ys - head_mean(ys)
        yn = dev * lax.rsqrt(head_mean(dev * dev) + LNX_EPS)
        o = yn * lw_ref[:, sl] + lb_ref[:, sl] + (b0_ref[:, sl] + b1_ref[:, sl])
        o_ref[:, sl] = (o * g[:, sl]).astype(o_ref.dtype)


def _rwkv_post(y0, y1, b0, b1, lora, p, *, tm=256):
    M, Rd = y0.shape
    tm = min(tm, M)
    assert M % tm == 0
    rows = pl.BlockSpec((tm, Rd), lambda i: (i, 0))
    vec = pl.BlockSpec((1, Rd), lambda i: (0, 0))
    return pl.pallas_call(
        _post_kernel,
        grid=(M // tm,),
        in_specs=[rows, rows, rows, rows, pl.BlockSpec((tm, 128), lambda i: (i, 2)),
                  pl.BlockSpec((G_LORA, Rd), lambda i: (0, 0)), vec, vec],
        out_specs=pl.BlockSpec((tm, Rd), lambda i: (i, 0)),
        out_shape=jax.ShapeDtypeStruct((M, Rd), BF16),
        compiler_params=_params(("parallel",)),
        name="rwkv_post",
    )(y0, y1, b0, b1, lora, p['g_up_bf16'], p['lnx_w'].reshape(1, Rd), p['lnx_b'].reshape(1, Rd))


def _rope_tables(T, n_cache):
    rows = T // GRID_W
    row = jnp.repeat(jnp.arange(rows, dtype=F32), GRID_W)
    col = jnp.tile(jnp.arange(GRID_W, dtype=F32), rows)
    inv_freq = jnp.power(ROPE_THETA, -jnp.arange(AXIS_PAIRS, dtype=F32) / AXIS_PAIRS)
    ar, ac = row[:, None] * inv_freq, col[:, None] * inv_freq
    cos = jnp.concatenate([jnp.cos(ar), jnp.cos(ar), jnp.cos(ac), jnp.cos(ac)], axis=1)
    sin = jnp.concatenate([-jnp.sin(ar), jnp.sin(ar), -jnp.sin(ac), jnp.sin(ac)], axis=1)
    cos = jnp.concatenate([cos, jnp.ones((n_cache, ROPE_DIM), F32)], axis=0)
    sin = jnp.concatenate([sin, jnp.zeros((n_cache, ROPE_DIM), F32)], axis=0)
    return jnp.tile(cos, (1, 2)), jnp.tile(sin, (1, 2))


def _trunk_layer(x, mod, p, cache):
    B, T, _ = x.shape
    M = B * T
    nb = mod.shape[0]
    shift1, scale1, gate1, shift2, scale2, gate2 = [
        m.reshape(nb, 1, D_MODEL) for m in jnp.split(mod, 6, axis=-1)]
    xf = x.reshape(M, D_MODEL)
    latent = cache is not None
    proj = functools.partial(_norm_matmul, xf, p['norm1'], scale1, shift1, rows_per_batch=T)

    q_tm = 512
    extra, extra_specs = [p['q_gain']], [lambda tm: pl.BlockSpec((1, 2 * Q_PAIR), lambda i, j: (0, 0))]
    if latent:
        cos, sin = _rope_tables(T, cache[0].shape[1])
        n_t = T // min(q_tm, T)
        tbl = lambda tm: pl.BlockSpec((tm, 128), lambda i, j: (i % n_t, 0))
        extra += [cos[:T], sin[:T]]
        extra_specs += [tbl, tbl]
    q, = proj(p['w_q'], tm=q_tm, tn=2 * Q_PAIR, epilogue=functools.partial(_epi_q, rope=latent),
              outs=[(2 * Q_PAIR, Q_DIM, BF16)], extra=extra, extra_specs=extra_specs, name="proj_q")
    z_ckv, z_kr, lora = proj(p['w_small'], tm=1024, tn=SMALL_COLS, epilogue=_epi_small,
                             outs=[(KV_RANK, KV_RANK, F32), (ROPE_DIM, ROPE_DIM, F32), (384, 384, F32)],
                             name="proj_small")
    z_rkv, = proj(p['w_rkv'], tm=1024, tn=1024, epilogue=_epi_plain,
                  outs=[(1024, 3 * R_DIM, F32)], name="proj_rkv")
    gates, = proj(p['w_gate'], tm=1024, tn=1024, epilogue=_epi_sigmoid,
                  outs=[(1024, 2 * D_MODEL, BF16)], name="proj_gate")

    if latent:
        ckv_ctx, kr_ctx, s0 = cache
        S = T + ckv_ctx.shape[1]
        ckv_all = jnp.concatenate([z_ckv.reshape(B, T, KV_RANK), ckv_ctx], axis=1).reshape(B * S, KV_RANK)
        kr_all = jnp.concatenate([z_kr.reshape(B, T, ROPE_DIM), kr_ctx], axis=1).reshape(B * S, ROPE_DIM)
        k, v = _keys_values(ckv_all, kr_all, p, (cos, sin), tm=S)
    else:
        S = T
        s0 = jnp.zeros((B, N_DIR, H_RWKV, HEAD_RWKV, HEAD_RWKV), F32)
        k, v = _keys_values(z_ckv, z_kr, p, None, tm=1024)
    o_mla = _attention(q.reshape(B, T, Q_DIM), k, v, S, pairs=2 if latent else 4).reshape(M, MLA_WIDTH)

    yb, s_final = _rwkv_scan(z_rkv.reshape(B, T, 3 * R_DIM), lora.reshape(B, T, 384), p, s0)
    o_rwkv = _rwkv_post(*[a.reshape(M, R_DIM) for a in yb], lora, p)

    merged = _branch_merge(o_mla, o_rwkv, gates, p['w_br_mla'], p['w_br_rwkv'])
    x1 = _out_proj(merged, p['w_out'], xf, gate1, T)
    x2 = _ffn(x1, p['norm2'], scale2, shift2, gate2, p['w_ff_in'], p['w_ff_out'], T)
    return (x2.reshape(B, T, D_MODEL),
            (z_ckv.reshape(B, T, KV_RANK), z_kr.reshape(B, T, ROPE_DIM), s_final))


def _split_w_in(w):
    offs = [sum(IN_SIZES[:i]) for i in range(len(IN_SIZES))]
    part = lambda i: w[:, offs[i]:offs[i] + IN_SIZES[i]]
    K = w.shape[0]
    wq = part(0).reshape(K, H_MLA // 2, 2, QK_HEAD)
    wq = jnp.concatenate([wq[..., :NOPE_DIM].reshape(K, H_MLA // 2, 2 * NOPE_DIM),
                          wq[..., NOPE_DIM:].reshape(K, H_MLA // 2, 2 * ROPE_DIM)], axis=-1)
    wd, ad = part(4), part(5)
    small = jnp.concatenate([part(1), part(2), jnp.zeros((K, 128 - ROPE_DIM), w.dtype),
                             wd[:, :W_LORA], ad[:, :A_LORA], wd[:, W_LORA:], ad[:, A_LORA:],
                             part(6)], axis=1)
    return (wq.reshape(K, Q_DIM).astype(BF16), small.astype(BF16), part(3).astype(BF16),
            part(7).astype(BF16))


def _layer_params(l, w_in, q_norm, w_kv_up, w_br_mla, w_br_rwkv, w_out, w_ff_in, w_ff_out, **small):
    zeros = jnp.zeros((N_DIR, W_LORA, R_DIM), F32)
    w_q, w_small, w_rkv, w_gate = _split_w_in(w_in[l])
    qg = q_norm[l] * (QK_HEAD ** -0.5)
    q_gain = jnp.concatenate([qg[:NOPE_DIM], qg[:NOPE_DIM], qg[NOPE_DIM:], qg[NOPE_DIM:]])
    p = {name: val[l] for name, val in small.items()}
    p.update({
        'w_q': w_q, 'w_small': w_small, 'w_rkv': w_rkv, 'w_gate': w_gate,
        'q_gain': jnp.tile(q_gain, 2).reshape(1, 2 * Q_PAIR),
        'w_kv_up': w_kv_up[l].astype(BF16),
        'w_up_pad': jnp.concatenate([small['w_up'][l], zeros], axis=1).astype(BF16),
        'a_up_pad': jnp.concatenate([zeros, small['a_up'][l]], axis=1).astype(BF16),
        'g_up_bf16': small['g_up'][l].astype(BF16),
        'w_br_mla': w_br_mla[l].astype(BF16), 'w_br_rwkv': w_br_rwkv[l].astype(BF16),
        'w_out': w_out[l].astype(BF16),
        'w_ff_in': w_ff_in[l].astype(BF16), 'w_ff_out': w_ff_out[l].astype(BF16),
    })
    return p


def kernel(x_prompt, x_sample, cache_mla_ckv, cache_mla_kr, state_rwkv, c, c_ctx,
           norm1, w_ada, b_ada, w_in, q_norm, kv_norm, w_kv_up, k_norm, conv_rkv,
           k_k, k_a, r_k, w0, w_up, a0, a_up, g_up, lnx_w, lnx_b,
           w_br_mla, w_br_rwkv, w_out, norm2, w_ff_in, w_ff_out):
    x_p, x_s = x_prompt, x_sample
    n_lat = c.shape[0]
    ckv_list, kr_list, st_list = [], [], []
    for l in range(DEPTH):
        p = _layer_params(l, w_in, q_norm, w_kv_up, w_br_mla, w_br_rwkv, w_out, w_ff_in, w_ff_out,
                          norm1=norm1, kv_norm=kv_norm, k_norm=k_norm, conv_rkv=conv_rkv, k_k=k_k,
                          k_a=k_a, r_k=r_k, w0=w0, w_up=w_up, a0=a0, a_up=a_up, g_up=g_up,
                          lnx_w=lnx_w, lnx_b=lnx_b, norm2=norm2)
        cond = jnp.concatenate([c, c_ctx[None]], axis=0)
        cond = jnp.pad(jax.nn.silu(cond), ((0, (-cond.shape[0]) % 16), (0, 0)))
        mod = _matmul(cond, w_ada[l], tm=16, tn=1024)[:n_lat + 1] + b_ada[l]
        x_p, (ckv_l, kr_l, st_l) = _trunk_layer(x_p, mod[n_lat:], p, None)
        ckv_list.append(ckv_l)
        kr_list.append(kr_l)
        st_list.append(st_l)
        x_s, _ = _trunk_layer(x_s, mod[:n_lat], p,
                              (cache_mla_ckv[:, l], cache_mla_kr[:, l], state_rwkv[:, l]))
    return (x_p, x_s, jnp.stack(ckv_list, axis=1), jnp.stack(kr_list, axis=1),
            jnp.stack(st_list, axis=1))
```

```python
import functools

import jax
import jax.numpy as jnp
from jax import lax
from jax.experimental import pallas as pl
from jax.experimental.pallas import tpu as pltpu

D_MODEL = 2048
DEPTH = 1
GRID_W = 64
H_MLA = 16
NOPE_DIM = 128
ROPE_DIM = 64
QK_HEAD = NOPE_DIM + ROPE_DIM
V_HEAD = 128
KV_RANK = 512
ROPE_THETA = 10000.0
AXIS_DIM = ROPE_DIM // 2
AXIS_PAIRS = AXIS_DIM // 2
MLA_WIDTH = H_MLA * V_HEAD
Q_DIM = H_MLA * QK_HEAD
HEAD_RWKV = 64
H_RWKV = D_MODEL // HEAD_RWKV
R_DIM = H_RWKV * HEAD_RWKV
W_LORA = 64
A_LORA = 64
G_LORA = 128
N_DIR = 2
LNX_EPS = 64e-5
D_FF = 4 * D_MODEL
EPS = 1e-6
IN_SIZES = (Q_DIM, KV_RANK, ROPE_DIM, 3 * R_DIM, N_DIR * W_LORA, N_DIR * A_LORA, G_LORA, 2 * D_MODEL)
IN_DIM = sum(IN_SIZES)

F32 = jnp.float32
BF16 = jnp.bfloat16
VMEM_LIMIT = 56 * 1024 * 1024
SCAN_CHUNK = 64
SCAN_PAIRS = 16


def _params(sem):
    return pltpu.CompilerParams(dimension_semantics=sem, vmem_limit_bytes=VMEM_LIMIT)


def _mm_kernel(a_ref, b_ref, o_ref):
    o_ref[...] = jnp.dot(a_ref[...].astype(BF16), b_ref[...].astype(BF16),
                         preferred_element_type=F32).astype(o_ref.dtype)


def _matmul(a, b, *, tm=512, tn=512, out_dtype=F32):
    M, K = a.shape
    N = b.shape[1]
    tm = min(tm, M)
    tn = min(tn, N)
    assert M % tm == 0 and N % tn == 0, (M, N, tm, tn)
    return pl.pallas_call(
        _mm_kernel,
        grid=(M // tm, N // tn),
        in_specs=[pl.BlockSpec((tm, K), lambda i, j: (i, 0)),
                  pl.BlockSpec((K, tn), lambda i, j: (0, j))],
        out_specs=pl.BlockSpec((tm, tn), lambda i, j: (i, j)),
        out_shape=jax.ShapeDtypeStruct((M, N), out_dtype),
        compiler_params=_params(("parallel", "arbitrary")),
        name="matmul",
    )(a, b)


def _modulated_norm(x, g, scale, shift):
    ms = jnp.mean(x * x, axis=-1, keepdims=True)
    return (x * lax.rsqrt(ms + EPS) * g) * (1.0 + scale) + shift


def _mod_index(nb, rows_per_batch, tm):
    if nb == 1:
        return lambda i, j: (0, 0, 0)
    assert rows_per_batch % tm == 0
    return lambda i, j: (i * tm // rows_per_batch, 0, 0)


def _row_tile(tm, M, nb, rows_per_batch):
    tm = min(tm, M, rows_per_batch) if nb > 1 else min(tm, M)
    assert M % tm == 0
    return tm


def _sigmoid(x):
    return 1.0 / (1.0 + jnp.exp(-x))


def _swap16(x):
    lane = lax.broadcasted_iota(jnp.int32, x.shape, 1)
    return jnp.where((lane & AXIS_PAIRS) == 0,
                     pltpu.roll(x, 128 - AXIS_PAIRS, 1), pltpu.roll(x, AXIS_PAIRS, 1))


def _epi_plain(acc, extra, outs, rows):
    outs[0][rows, :] = acc.astype(outs[0].dtype)


def _epi_sigmoid(acc, extra, outs, rows):
    outs[0][rows, :] = _sigmoid(acc).astype(outs[0].dtype)


SMALL_LORA = KV_RANK + 128
SMALL_COLS = SMALL_LORA + 3 * 128


def _epi_small(acc, extra, outs, rows):
    ckv_ref, kr_ref, lora_ref = outs
    ckv_ref[rows, :] = acc[:, :KV_RANK]
    kr_ref[rows, :] = acc[:, KV_RANK:KV_RANK + ROPE_DIM]
    o = SMALL_LORA
    wa = acc[:, o:o + 256]
    lane = lax.broadcasted_iota(jnp.int32, wa.shape, 1)
    wa = jnp.where((lane & W_LORA) == 0, jnp.tanh(wa), wa)
    lora_ref[rows, :] = jnp.concatenate([wa, _sigmoid(acc[:, o + 256:o + 384])], axis=1)


Q_PAIR = 2 * QK_HEAD


def _epi_q(acc, extra, outs, rows, *, rope):
    g_ref = extra[0]
    tm, tn = acc.shape
    lo = lax.broadcasted_iota(jnp.int32, (tm, 128), 1) < ROPE_DIM
    parts = []
    for p in range(tn // Q_PAIR):
        z = acc[:, p * Q_PAIR:(p + 1) * Q_PAIR]
        g = g_ref[:, p * Q_PAIR:(p + 1) * Q_PAIR]
        n0, n1, rp = z[:, :128], z[:, 128:256], z[:, 256:]
        rp2 = rp * rp
        s0 = (jnp.sum(n0 * n0, axis=-1, keepdims=True)
              + jnp.sum(jnp.where(lo, rp2, 0.0), axis=-1, keepdims=True))
        s1 = (jnp.sum(n1 * n1, axis=-1, keepdims=True)
              + jnp.sum(jnp.where(lo, 0.0, rp2), axis=-1, keepdims=True))
        r0 = lax.rsqrt(s0 * (1.0 / QK_HEAD) + EPS)
        r1 = lax.rsqrt(s1 * (1.0 / QK_HEAD) + EPS)
        qr = rp * jnp.where(lo, r0, r1) * g[:, 256:]
        if rope:
            qr = qr * extra[1][rows, :] + _swap16(qr) * extra[2][rows, :]
        parts += [n0 * r0 * g[:, :128], n1 * r1 * g[:, 128:256], qr]
    outs[0][rows, :] = jnp.concatenate(parts, axis=1).astype(outs[0].dtype)


def _norm_mm_kernel(*refs, n_extra, n_out, epilogue, row_split):
    x_ref, g_ref, sc_ref, sh_ref, w_ref = refs[:5]
    extra = refs[5:5 + n_extra]
    outs = refs[5 + n_extra:5 + n_extra + n_out]
    h_scr = refs[-1]

    @pl.when(pl.program_id(1) == 0)
    def _():
        h = _modulated_norm(x_ref[...], g_ref[...], sc_ref[0], sh_ref[0])
        h_scr[...] = h.astype(BF16)

    step = h_scr.shape[0] // row_split
    for r in range(row_split):
        rows = pl.ds(r * step, step)
        epilogue(jnp.dot(h_scr[rows, :], w_ref[...], preferred_element_type=F32), extra, outs, rows)


def _norm_matmul(x, g, scale, shift, w, rows_per_batch, *, tm, tn, epilogue, outs,
                 extra=(), extra_specs=(), row_split=2, name):
    M, K = x.shape
    N = w.shape[1]
    nb = scale.shape[0]
    tm = _row_tile(tm, M, nb, rows_per_batch)
    assert N % tn == 0 and tm % (16 * row_split) == 0
    midx = _mod_index(nb, rows_per_batch, tm)
    res = pl.pallas_call(
        functools.partial(_norm_mm_kernel, n_extra=len(extra), n_out=len(outs), epilogue=epilogue,
                          row_split=row_split),
        grid=(M // tm, N // tn),
        in_specs=[pl.BlockSpec((tm, K), lambda i, j: (i, 0)),
                  pl.BlockSpec((1, K), lambda i, j: (0, 0)),
                  pl.BlockSpec((1, 1, K), midx),
                  pl.BlockSpec((1, 1, K), midx),
                  pl.BlockSpec((K, tn), lambda i, j: (0, j))] + [s(tm) for s in extra_specs],
        out_specs=[pl.BlockSpec((tm, c), lambda i, j: (i, j)) for c, _, _ in outs],
        out_shape=[jax.ShapeDtypeStruct((M, n), dt) for _, n, dt in outs],
        scratch_shapes=[pltpu.VMEM((tm, K), BF16)],
        compiler_params=_params(("parallel", "arbitrary")),
        name=name,
    )(x, g.reshape(1, K), scale, shift, w, *extra)
    return res


K_WIDTH = 2 * NOPE_DIM
KV_ROW_SPLIT = 4


def _kv_kernel(*refs, rope):
    if rope:
        (ckv_ref, kr_ref, gkv_ref, gk_ref, cos_ref, sin_ref, w_ref, k_ref, v_ref,
         a_scr, kr_scr, ss_scr) = refs
    else:
        ckv_ref, kr_ref, gkv_ref, gk_ref, w_ref, k_ref, v_ref, a_scr, kr_scr, ss_scr = refs
    h = pl.program_id(1)

    @pl.when(h == 0)
    def _():
        c = ckv_ref[...]
        a = c * lax.rsqrt(jnp.mean(c * c, axis=-1, keepdims=True) + EPS) * gkv_ref[...]
        a_scr[...] = a.astype(BF16)
        kr = kr_ref[...]
        ss_scr[...] = jnp.broadcast_to(0.5 * jnp.sum(kr * kr, axis=-1, keepdims=True), ss_scr.shape)
        krg = kr * gk_ref[:, NOPE_DIM:]
        if rope:
            krg = krg * cos_ref[...] + _swap16(krg) * sin_ref[...]
        kr_scr[...] = krg

    step = a_scr.shape[0] // KV_ROW_SPLIT
    for r in range(KV_ROW_SPLIT):
        rows = pl.ds(r * step, step)
        acc = jnp.dot(a_scr[rows, :], w_ref[...], preferred_element_type=F32)
        kn = acc[:, :NOPE_DIM]
        ssq = jnp.sum(kn * kn, axis=-1, keepdims=True) + ss_scr[rows, :1]
        rstd = lax.rsqrt(ssq * (1.0 / QK_HEAD) + EPS)
        lane = lax.broadcasted_iota(jnp.int32, kn.shape, 1)
        mine = (lane >> 6) == (h & 1)
        krh = jnp.where(mine, kr_scr[rows, :] * rstd, 0.0)
        k_ref[0, rows, :] = jnp.concatenate([kn * rstd * gk_ref[:, :NOPE_DIM], krh], axis=1).astype(BF16)
        v_ref[0, rows, :] = acc[:, NOPE_DIM:].astype(BF16)


def _keys_values(ckv, kr, p, tables, *, tm):
    Mk = ckv.shape[0]
    tm = min(tm, Mk)
    assert Mk % tm == 0 and tm % (16 * KV_ROW_SPLIT) == 0
    rope = tables is not None
    kr2 = jnp.concatenate([kr, kr], axis=-1)
    gk = jnp.concatenate([p['k_norm'], p['k_norm'][NOPE_DIM:]]).reshape(1, K_WIDTH)
    row = lambda i, h: (i, 0)
    fixed = lambda i, h: (0, 0)
    in_specs = [pl.BlockSpec((tm, KV_RANK), row), pl.BlockSpec((tm, 128), row),
                pl.BlockSpec((1, KV_RANK), fixed), pl.BlockSpec((1, K_WIDTH), fixed)]
    args = [ckv, kr2, p['kv_norm'].reshape(1, KV_RANK), gk]
    if rope:
        in_specs += [pl.BlockSpec((tm, 128), fixed), pl.BlockSpec((tm, 128), fixed)]
        args += list(tables)
    in_specs.append(pl.BlockSpec((KV_RANK, NOPE_DIM + V_HEAD), lambda i, h: (0, h)))
    args.append(p['w_kv_up'])
    return pl.pallas_call(
        functools.partial(_kv_kernel, rope=rope),
        grid=(Mk // tm, H_MLA),
        in_specs=in_specs,
        out_specs=[pl.BlockSpec((1, tm, K_WIDTH), lambda i, h: (h, i, 0)),
                   pl.BlockSpec((1, tm, V_HEAD), lambda i, h: (h, i, 0))],
        out_shape=[jax.ShapeDtypeStruct((H_MLA, Mk, K_WIDTH), BF16),
                   jax.ShapeDtypeStruct((H_MLA, Mk, V_HEAD), BF16)],
        scratch_shapes=[pltpu.VMEM((tm, KV_RANK), BF16), pltpu.VMEM((tm, 128), F32),
                        pltpu.VMEM((tm, 128), F32)],
        compiler_params=_params(("parallel", "arbitrary")),
        name="keys_values",
    )(*args)


def _attn_kernel(q_ref, k_ref, v_ref, o_ref, *, pairs):
    for pr in range(pairs):
        q = q_ref[0, :, pr * Q_PAIR:(pr + 1) * Q_PAIR]
        qr = q[:, 2 * NOPE_DIM:]
        for h in range(2):
            hh = 2 * pr + h
            qh = jnp.concatenate([q[:, h * NOPE_DIM:(h + 1) * NOPE_DIM], qr], axis=1)
            s = lax.dot_general(qh, k_ref[hh], (((1,), (1,)), ((), ())), preferred_element_type=F32)
            m = jnp.max(s, axis=-1, keepdims=True)
            p = jnp.exp(s - m)
            l = jnp.sum(p, axis=-1, keepdims=True)
            o = jnp.dot(p.astype(BF16), v_ref[hh], preferred_element_type=F32) / l
            o_ref[0, :, hh * V_HEAD:(hh + 1) * V_HEAD] = o.astype(o_ref.dtype)


def _attention(q, k, v, S, *, tq=256, pairs):
    B, T, _ = q.shape
    tq = min(tq, T)
    hp = 2 * pairs
    return pl.pallas_call(
        functools.partial(_attn_kernel, pairs=pairs),
        grid=(B, H_MLA // hp, T // tq),
        in_specs=[pl.BlockSpec((1, tq, pairs * Q_PAIR), lambda b, p, i: (b, i, p)),
                  pl.BlockSpec((hp, S, K_WIDTH), lambda b, p, i: (p, b, 0)),
                  pl.BlockSpec((hp, S, V_HEAD), lambda b, p, i: (p, b, 0))],
        out_specs=pl.BlockSpec((1, tq, hp * V_HEAD), lambda b, p, i: (b, i, p)),
        out_shape=jax.ShapeDtypeStruct((B, T, MLA_WIDTH), BF16),
        compiler_params=_params(("parallel", "parallel", "arbitrary")),
        name="attention",
    )(q, k, v)


def _branch_kernel(a_ref, b_ref, ga_ref, gb_ref, wa_ref, wb_ref, o_ref):
    ya = jnp.dot(a_ref[...], wa_ref[...], preferred_element_type=F32)
    yb = jnp.dot(b_ref[...], wb_ref[...], preferred_element_type=F32)
    o_ref[...] = (ga_ref[...].astype(F32) * ya + gb_ref[...].astype(F32) * yb).astype(o_ref.dtype)


def _branch_merge(a, b, gates, wa, wb, *, tm=1024, tn=1024):
    M, K = a.shape
    N = wa.shape[1]
    tm = min(tm, M)
    nj = N // tn
    return pl.pallas_call(
        _branch_kernel,
        grid=(M // tm, nj),
        in_specs=[pl.BlockSpec((tm, K), lambda i, j: (i, 0)),
                  pl.BlockSpec((tm, K), lambda i, j: (i, 0)),
                  pl.BlockSpec((tm, tn), lambda i, j: (i, j)),
                  pl.BlockSpec((tm, tn), lambda i, j: (i, j + nj)),
                  pl.BlockSpec((K, tn), lambda i, j: (0, j)),
                  pl.BlockSpec((K, tn), lambda i, j: (0, j))],
        out_specs=pl.BlockSpec((tm, tn), lambda i, j: (i, j)),
        out_shape=jax.ShapeDtypeStruct((M, N), BF16),
        compiler_params=_params(("parallel", "arbitrary")),
        name="branch_merge",
    )(a, b, gates, gates, wa, wb)


def _resid_kernel(m_ref, w_ref, x_ref, gt_ref, o_ref):
    o_ref[...] = x_ref[...] + gt_ref[0] * jnp.dot(m_ref[...], w_ref[...],
                                                  preferred_element_type=F32)


def _out_proj(m, w, x, gate, rows_per_batch, *, tm=1024, tn=1024):
    M, K = m.shape
    N = w.shape[1]
    nb = gate.shape[0]
    tm = _row_tile(tm, M, nb, rows_per_batch)
    if nb == 1:
        gidx = lambda i, j: (0, 0, j)
    else:
        gidx = lambda i, j: (i * tm // rows_per_batch, 0, j)
    return pl.pallas_call(
        _resid_kernel,
        grid=(M // tm, N // tn),
        in_specs=[pl.BlockSpec((tm, K), lambda i, j: (i, 0)),
                  pl.BlockSpec((K, tn), lambda i, j: (0, j)),
                  pl.BlockSpec((tm, tn), lambda i, j: (i, j)),
                  pl.BlockSpec((1, 1, tn), gidx)],
        out_specs=pl.BlockSpec((tm, tn), lambda i, j: (i, j)),
        out_shape=jax.ShapeDtypeStruct((M, N), F32),
        compiler_params=_params(("parallel", "arbitrary")),
        name="out_proj",
    )(m, w, x, gate)


def _ffn_kernel(x_ref, g_ref, sc_ref, sh_ref, gt_ref, w1_ref, w2_ref, o_ref, h_scr, acc_scr):
    f = pl.program_id(1)

    @pl.when(f == 0)
    def _():
        h = _modulated_norm(x_ref[...], g_ref[...], sc_ref[0], sh_ref[0])
        h_scr[...] = h.astype(BF16)
        acc_scr[...] = jnp.zeros_like(acc_scr)

    u = jnp.dot(h_scr[...], w1_ref[...], preferred_element_type=F32)
    u = jnp.square(jnp.maximum(u, 0.0))
    acc_scr[...] += jnp.dot(u.astype(BF16), w2_ref[...], preferred_element_type=F32)

    @pl.when(f == pl.num_programs(1) - 1)
    def _():
        o_ref[...] = x_ref[...] + gt_ref[0] * acc_scr[...]


def _ffn(x, g, scale, shift, gate, w1, w2, rows_per_batch, *, tm=512, tf=1024):
    M, K = x.shape
    F = w1.shape[1]
    nb = scale.shape[0]
    tm = _row_tile(tm, M, nb, rows_per_batch)
    assert F % tf == 0
    midx = _mod_index(nb, rows_per_batch, tm)
    return pl.pallas_call(
        _ffn_kernel,
        grid=(M // tm, F // tf),
        in_specs=[pl.BlockSpec((tm, K), lambda i, j: (i, 0)),
                  pl.BlockSpec((1, K), lambda i, j: (0, 0)),
                  pl.BlockSpec((1, 1, K), midx),
                  pl.BlockSpec((1, 1, K), midx),
                  pl.BlockSpec((1, 1, K), midx),
                  pl.BlockSpec((K, tf), lambda i, j: (0, j)),
                  pl.BlockSpec((tf, K), lambda i, j: (j, 0))],
        out_specs=pl.BlockSpec((tm, K), lambda i, j: (i, 0)),
        out_shape=jax.ShapeDtypeStruct((M, K), F32),
        scratch_shapes=[pltpu.VMEM((tm, K), BF16), pltpu.VMEM((tm, K), F32)],
        compiler_params=_params(("parallel", "arbitrary")),
        name="ffn",
    )(x, g.reshape(1, K), scale, shift, gate, w1, w2)


def _bdot(a, b, dims=((1,), (0,))):
    return lax.dot_general(a.astype(BF16), b.astype(BF16), (dims, ((), ())),
                           preferred_element_type=F32)


_NT = ((1,), (1,))
_TN = ((0,), (0,))
PAIR = 2 * HEAD_RWKV


def _scan_kernel(z0_ref, zp0_ref, zn0_ref, lo0_ref, z1_ref, zp1_ref, zn1_ref, lo1_ref,
                 cw_ref, kkg_ref, ka_ref, rk_ref, w0_ref, a0_ref, wup_ref, aup_ref, s0_ref,
                 y0_ref, y1_ref, bon0_ref, bon1_ref, sf_ref, s_scr, *, chunk, pairs, n_chunks):
    c = pl.program_id(1)
    C = chunk
    N = HEAD_RWKV

    @pl.when(c == 0)
    def _():
        for d in range(N_DIR):
            for p in range(pairs):
                s_scr[d, p] = jnp.concatenate([s0_ref[0, d, 2 * p], s0_ref[0, d, 2 * p + 1]], axis=1)

    row = lax.broadcasted_iota(jnp.int32, (C, PAIR), 0)
    lane = lax.broadcasted_iota(jnp.int32, (C, PAIR), 1)
    col = lane & (N - 1)
    eye = row == col
    head0 = lane < N
    ti = lax.broadcasted_iota(jnp.int32, (C, C), 0)
    tj = lax.broadcasted_iota(jnp.int32, (C, C), 1)

    def bd(x):
        return jnp.concatenate([jnp.where(head0, x, 0.0), jnp.where(head0, 0.0, x)],
                               axis=0).astype(BF16)

    def diag_blocks(x):
        return jnp.where(head0, x[:N], x[N:])

    def head_sum(x):
        s0 = jnp.sum(jnp.where(head0, x, 0.0), axis=-1, keepdims=True)
        s1 = jnp.sum(jnp.where(head0, 0.0, x), axis=-1, keepdims=True)
        return jnp.where(head0, s0, s1)

    def direction(d, z_ref, zp_ref, zn_ref, lo_ref, y_ref, bon_ref):
        tt = c if d == 0 else n_chunks - 1 - c
        first = tt == 0
        last = tt == n_chunks - 1
        if d == 0:
            incl, strict, tri = row >= col, row > col, ti >= tj
        else:
            incl, strict, tri = row <= col, row < col, ti <= tj
        tri = jnp.where(tri, 1.0, 0.0).astype(BF16)
        lora = lo_ref[0].astype(BF16)

        def conv(off, sl):
            zs = slice(off + sl.start, off + sl.stop)
            zc = z_ref[0, :, zs]
            before = jnp.where(first, 0.0, zp_ref[0, 7:8, zs])
            after = jnp.where(last, 0.0, zn_ref[0, 0:1, zs])
            zm = jnp.where(row == 0, before, pltpu.roll(zc, 1, 0))
            zq = jnp.where(row == C - 1, after, pltpu.roll(zc, C - 1, 0))
            return zm * cw_ref[0:1, zs] + zc * cw_ref[1:2, zs] + zq * cw_ref[2:3, zs]

        def s_lora(st, sl, p):
            u = w0_ref[d, :, sl] + jnp.dot(lora, wup_ref[d, :, sl], preferred_element_type=F32)
            softplus = jnp.maximum(-u, 0.0) + jnp.log(1.0 + jnp.exp(-jnp.abs(u)))
            st['ld'] = -jnp.exp(-softplus - 0.5)
            st['a'] = _sigmoid(a0_ref[d, :, sl] + jnp.dot(lora, aup_ref[d, :, sl],
                                                         preferred_element_type=F32))

        def s_cum(st, sl, p):
            ld = st['ld']
            ld_hi = ld.astype(BF16)
            rem = ld - ld_hi.astype(F32)
            ld_mid = rem.astype(BF16)
            ld_lo = (rem - ld_mid.astype(F32)).astype(BF16)
            st['cum'] = (jnp.dot(tri, ld_hi, preferred_element_type=F32)
                         + jnp.dot(tri, ld_mid, preferred_element_type=F32)
                         + jnp.dot(tri, ld_lo, preferred_element_type=F32))
            st['tot'] = jnp.sum(ld, axis=0, keepdims=True)

        def s_conv(st, sl, p):
            st['R'] = conv(0, sl)
            st['Kraw'] = conv(R_DIM, sl)
            st['V'] = conv(2 * R_DIM, sl)

        def s_keys(st, sl, p):
            kx = st['Kraw'] * kkg_ref[:, sl]
            kk = kx * lax.rsqrt(head_sum(kx * kx) + 1e-12)
            a = st.pop('a')
            st['K'] = st.pop('Kraw') * (1.0 + (a - 1.0) * ka_ref[:, sl])
            st['Bv'] = kk * a
            st['kk'] = kk
            bon_ref[0, :, sl] = head_sum(st['R'] * st['K'] * rk_ref[:, sl]) * st['V']

        def s_decay(st, sl, p):
            cum, ld, tot = st.pop('cum'), st.pop('ld'), st['tot']
            e_neg = jnp.exp(-cum)
            e_end = jnp.exp(tot - cum)
            Bv, K = st.pop('Bv'), st.pop('K')
            st['At'] = -st.pop('kk') * jnp.exp(cum - ld)
            st['Rt'] = st.pop('R') * jnp.exp(cum)
            st['b_d'] = Bv * e_end
            st['k_d'] = K * e_end
            st['Bt'] = Bv * e_neg
            st['Kt'] = K * e_neg

        def s_scores(st, sl, p):
            st['bdv'] = bd(st['V'])
            st['sc'] = _bdot(jnp.concatenate([st['At'], st['Rt']], axis=0),
                             jnp.concatenate([bd(st.pop('Bt')), bd(st.pop('Kt'))], axis=0),
                             _NT)

        def s_masks(st, sl, p):
            sc = st.pop('sc')
            L = jnp.where(strict, sc[:C, :PAIR], 0.0)
            st['p_rb'] = jnp.where(incl, sc[C:, :PAIR], 0.0)
            st['p_rk'] = jnp.where(incl, sc[C:, PAIR:], 0.0)
            st['akv'] = _bdot(jnp.where(strict, sc[:C, PAIR:], 0.0), st['bdv'])
            st['x'] = jnp.where(eye, 1.0, 0.0) + L
            st['lp'] = _bdot(L, bd(L))

        def s_double(st, sl, p):
            o = _bdot(st['lp'], jnp.concatenate([bd(st['x']), bd(st['lp'])], axis=1))
            st['x'] = st['x'] + o[:, :PAIR]
            st['lp'] = o[:, PAIR:]

        def s_double_last(st, sl, p):
            st['x'] = st['x'] + _bdot(st.pop('lp'), bd(st['x']))

        def s_solve(st, sl, p):
            z = _bdot(st.pop('x'), jnp.concatenate([bd(st.pop('At')), bd(st.pop('akv'))], axis=1))
            st['a_p'], st['u_loc'] = z[:, :PAIR], z[:, PAIR:]

        def s_local(st, sl, p):
            a_p, u_loc = st.pop('a_p'), st.pop('u_loc')
            f = _bdot(st.pop('p_rb'), jnp.concatenate([bd(a_p), bd(u_loc)], axis=1))
            st['r_p'] = st.pop('Rt') + f[:, :PAIR]
            st['y_loc'] = f[:, PAIR:] + _bdot(st.pop('p_rk'), st.pop('bdv'))
            st['g'] = diag_blocks(_bdot(a_p, st['b_d'], _TN))
            st['s_loc'] = diag_blocks(_bdot(jnp.concatenate([u_loc, st.pop('V')], axis=0),
                                            jnp.concatenate([st.pop('b_d'), st.pop('k_d')], axis=0),
                                            _TN))

        def s_state(st, sl, p):
            S = s_scr[d, p]
            y_ref[0, :, sl] = _bdot(st.pop('r_p'), bd(S), _NT) + st.pop('y_loc')
            s_scr[d, p] = S * jnp.exp(st.pop('tot')) + _bdot(S, bd(st.pop('g'))) + st.pop('s_loc')

        prep = [s_lora, s_cum, s_conv, s_keys, s_decay]
        matrix = [s_scores, s_masks]
        n = 2
        while n < C:
            matrix.append(s_double if 2 * n < C else s_double_last)
            n *= 2
        matrix += [s_solve, s_local, s_state]
        return prep, matrix

    sls = [slice(p * PAIR, (p + 1) * PAIR) for p in range(pairs)]
    prep0, mat0 = direction(0, z0_ref, zp0_ref, zn0_ref, lo0_ref, y0_ref, bon0_ref)
    prep1, mat1 = direction(1, z1_ref, zp1_ref, zn1_ref, lo1_ref, y1_ref, bon1_ref)
    st0 = [dict() for _ in range(pairs)]
    st1 = [dict() for _ in range(pairs)]

    for fn in prep0:
        for p in range(pairs):
            fn(st0[p], sls[p], p)
    prep1_calls = [(fn, p) for fn in prep1 for p in range(pairs)]
    per = -(-len(prep1_calls) // len(mat0))
    for k, fn in enumerate(mat0):
        for p in range(pairs):
            fn(st0[p], sls[p], p)
        for pf, p in prep1_calls[k * per:(k + 1) * per]:
            pf(st1[p], sls[p], p)
    for fn in mat1:
        for p in range(pairs):
            fn(st1[p], sls[p], p)

    @pl.when(c == n_chunks - 1)
    def _():
        for d in range(N_DIR):
            for p in range(pairs):
                sf_ref[0, d, 2 * p] = s_scr[d, p, :, :N]
                sf_ref[0, d, 2 * p + 1] = s_scr[d, p, :, N:]


def _rwkv_scan(z_rkv, lora, p, s0):
    B, T, _ = z_rkv.shape
    C = SCAN_CHUNK
    nC = T // C
    n_pairs = H_RWKV // 2
    N = HEAD_RWKV
    hb = C // 8

    fwd = lambda cc: cc
    bwd = lambda cc: nC - 1 - cc

    def chunk_specs(tm, dd):
        return [
            pl.BlockSpec((1, C, 3 * R_DIM), lambda bb, cc: (bb, tm(cc), 0)),
            pl.BlockSpec((1, 8, 3 * R_DIM), lambda bb, cc: (bb, jnp.maximum(tm(cc) * hb - 1, 0), 0)),
            pl.BlockSpec((1, 8, 3 * R_DIM),
                         lambda bb, cc: (bb, jnp.minimum((tm(cc) + 1) * hb, T // 8 - 1), 0)),
            pl.BlockSpec((1, C, 128), lambda bb, cc: (bb, tm(cc), dd))]

    row3 = lambda a: a.reshape(1, -1)
    whole = lambda shape: pl.BlockSpec(shape, lambda bb, cc: (0,) * len(shape))
    vec = whole((1, R_DIM))
    out0 = pl.BlockSpec((1, C, R_DIM), lambda bb, cc: (bb, fwd(cc), 0))
    out1 = pl.BlockSpec((1, C, R_DIM), lambda bb, cc: (bb, bwd(cc), 0))
    state = pl.BlockSpec((1, N_DIR, H_RWKV, N, N), lambda bb, cc: (bb, 0, 0, 0, 0))
    in_specs = (chunk_specs(fwd, 0) + chunk_specs(bwd, 1)
                + [whole((3, 3 * R_DIM)), vec, vec, vec, whole((N_DIR, 1, R_DIM)), whole((N_DIR, 1, R_DIM)),
                   whole((N_DIR, 128, R_DIM)), whole((N_DIR, 128, R_DIM)), state])
    ydir = jax.ShapeDtypeStruct((B, T, R_DIM), F32)
    y0, y1, b0, b1, s_fin = pl.pallas_call(
        functools.partial(_scan_kernel, chunk=C, pairs=n_pairs, n_chunks=nC),
        grid=(B, nC),
        in_specs=in_specs,
        out_specs=[out0, out1, out0, out1, state],
        out_shape=[ydir, ydir, ydir, ydir, jax.ShapeDtypeStruct((B, N_DIR, H_RWKV, N, N), F32)],
        scratch_shapes=[pltpu.VMEM((N_DIR, n_pairs, N, PAIR), F32)],
        compiler_params=_params(("parallel", "arbitrary")),
        name="rwkv_scan",
    )(z_rkv, z_rkv, z_rkv, lora, z_rkv, z_rkv, z_rkv, lora, p['conv_rkv'], row3(p['k_k']),
      row3(p['k_a']), row3(p['r_k']), p['w0'].reshape(N_DIR, 1, R_DIM), p['a0'].reshape(N_DIR, 1, R_DIM),
      p['w_up_pad'], p['a_up_pad'], s0)
    return (y0, y1, b0, b1), s_fin


def _post_kernel(y0_ref, y1_ref, b0_ref, b1_ref, gd_ref, gup_ref, lw_ref, lb_ref, o_ref):
    tm = o_ref.shape[0]
    g = jnp.dot(gd_ref[...].astype(BF16), gup_ref[...], preferred_element_type=F32)
    head0 = lax.broadcasted_iota(jnp.int32, (tm, PAIR), 1) < HEAD_RWKV

    def head_mean(x):
        s0 = jnp.sum(jnp.where(head0, x, 0.0), axis=-1, keepdims=True)
        s1 = jnp.sum(jnp.where(head0, 0.0, x), axis=-1, keepdims=True)
        return jnp.where(head0, s0, s1) * (1.0 / HEAD_RWKV)

    for p in range(R_DIM // PAIR):
        sl = slice(p * PAIR, (p + 1) * PAIR)
        ys = y0_ref[:, sl] + y1_ref[:, sl]
        dev = ys - head_mean(ys)
        yn = dev * lax.rsqrt(head_mean(dev * dev) + LNX_EPS)
        o = yn * lw_ref[:, sl] + lb_ref[:, sl] + (b0_ref[:, sl] + b1_ref[:, sl])
        o_ref[:, sl] = (o * g[:, sl]).astype(o_ref.dtype)


def _rwkv_post(y0, y1, b0, b1, lora, p, *, tm=256):
    M, Rd = y0.shape
    tm = min(tm, M)
    assert M % tm == 0
    rows = pl.BlockSpec((tm, Rd), lambda i: (i, 0))
    vec = pl.BlockSpec((1, Rd), lambda i: (0, 0))
    return pl.pallas_call(
        _post_kernel,
        grid=(M // tm,),
        in_specs=[rows, rows, rows, rows, pl.BlockSpec((tm, 128), lambda i: (i, 2)),
                  pl.BlockSpec((G_LORA, Rd), lambda i: (0, 0)), vec, vec],
        out_specs=pl.BlockSpec((tm, Rd), lambda i: (i, 0)),
        out_shape=jax.ShapeDtypeStruct((M, Rd), BF16),
        compiler_params=_params(("parallel",)),
        name="rwkv_post",
    )(y0, y1, b0, b1, lora, p['g_up_bf16'], p['lnx_w'].reshape(1, Rd), p['lnx_b'].reshape(1, Rd))


def _rope_tables(T, n_cache):
    rows = T // GRID_W
    row = jnp.repeat(jnp.arange(rows, dtype=F32), GRID_W)
    col = jnp.tile(jnp.arange(GRID_W, dtype=F32), rows)
    inv_freq = jnp.power(ROPE_THETA, -jnp.arange(AXIS_PAIRS, dtype=F32) / AXIS_PAIRS)
    ar, ac = row[:, None] * inv_freq, col[:, None] * inv_freq
    cos = jnp.concatenate([jnp.cos(ar), jnp.cos(ar), jnp.cos(ac), jnp.cos(ac)], axis=1)
    sin = jnp.concatenate([-jnp.sin(ar), jnp.sin(ar), -jnp.sin(ac), jnp.sin(ac)], axis=1)
    cos = jnp.concatenate([cos, jnp.ones((n_cache, ROPE_DIM), F32)], axis=0)
    sin = jnp.concatenate([sin, jnp.zeros((n_cache, ROPE_DIM), F32)], axis=0)
    return jnp.tile(cos, (1, 2)), jnp.tile(sin, (1, 2))


def _trunk_layer(x, mod, p, cache):
    B, T, _ = x.shape
    M = B * T
    nb = mod.shape[0]
    shift1, scale1, gate1, shift2, scale2, gate2 = [
        m.reshape(nb, 1, D_MODEL) for m in jnp.split(mod, 6, axis=-1)]
    xf = x.reshape(M, D_MODEL)
    latent = cache is not None
    proj = functools.partial(_norm_matmul, xf, p['norm1'], scale1, shift1, rows_per_batch=T)

    q_tm = 512
    extra, extra_specs = [p['q_gain']], [lambda tm: pl.BlockSpec((1, 2 * Q_PAIR), lambda i, j: (0, 0))]
    if latent:
        cos, sin = _rope_tables(T, cache[0].shape[1])
        n_t = T // min(q_tm, T)
        tbl = lambda tm: pl.BlockSpec((tm, 128), lambda i, j: (i % n_t, 0))
        extra += [cos[:T], sin[:T]]
        extra_specs += [tbl, tbl]
    q, = proj(p['w_q'], tm=q_tm, tn=2 * Q_PAIR, epilogue=functools.partial(_epi_q, rope=latent),
              outs=[(2 * Q_PAIR, Q_DIM, BF16)], extra=extra, extra_specs=extra_specs, name="proj_q")
    z_ckv, z_kr, lora = proj(p['w_small'], tm=1024, tn=SMALL_COLS, epilogue=_epi_small,
                             outs=[(KV_RANK, KV_RANK, F32), (ROPE_DIM, ROPE_DIM, F32), (384, 384, F32)],
                             name="proj_small")
    z_rkv, = proj(p['w_rkv'], tm=1024, tn=1024, epilogue=_epi_plain,
                  outs=[(1024, 3 * R_DIM, F32)], name="proj_rkv")
    gates, = proj(p['w_gate'], tm=1024, tn=1024, epilogue=_epi_sigmoid,
                  outs=[(1024, 2 * D_MODEL, BF16)], name="proj_gate")

    if latent:
        ckv_ctx, kr_ctx, s0 = cache
        S = T + ckv_ctx.shape[1]
        ckv_all = jnp.concatenate([z_ckv.reshape(B, T, KV_RANK), ckv_ctx], axis=1).reshape(B * S, KV_RANK)
        kr_all = jnp.concatenate([z_kr.reshape(B, T, ROPE_DIM), kr_ctx], axis=1).reshape(B * S, ROPE_DIM)
        k, v = _keys_values(ckv_all, kr_all, p, (cos, sin), tm=S)
    else:
        S = T
        s0 = jnp.zeros((B, N_DIR, H_RWKV, HEAD_RWKV, HEAD_RWKV), F32)
        k, v = _keys_values(z_ckv, z_kr, p, None, tm=1024)
    o_mla = _attention(q.reshape(B, T, Q_DIM), k, v, S, pairs=2 if latent else 4).reshape(M, MLA_WIDTH)

    yb, s_final = _rwkv_scan(z_rkv.reshape(B, T, 3 * R_DIM), lora.reshape(B, T, 384), p, s0)
    o_rwkv = _rwkv_post(*[a.reshape(M, R_DIM) for a in yb], lora, p)

    merged = _branch_merge(o_mla, o_rwkv, gates, p['w_br_mla'], p['w_br_rwkv'])
    x1 = _out_proj(merged, p['w_out'], xf, gate1, T)
    x2 = _ffn(x1, p['norm2'], scale2, shift2, gate2, p['w_ff_in'], p['w_ff_out'], T)
    return (x2.reshape(B, T, D_MODEL),
            (z_ckv.reshape(B, T, KV_RANK), z_kr.reshape(B, T, ROPE_DIM), s_final))


def _split_w_in(w):
    offs = [sum(IN_SIZES[:i]) for i in range(len(IN_SIZES))]
    part = lambda i: w[:, offs[i]:offs[i] + IN_SIZES[i]]
    K = w.shape[0]
    wq = part(0).reshape(K, H_MLA // 2, 2, QK_HEAD)
    wq = jnp.concatenate([wq[..., :NOPE_DIM].reshape(K, H_MLA // 2, 2 * NOPE_DIM),
                          wq[..., NOPE_DIM:].reshape(K, H_MLA // 2, 2 * ROPE_DIM)], axis=-1)
    wd, ad = part(4), part(5)
    small = jnp.concatenate([part(1), part(2), jnp.zeros((K, 128 - ROPE_DIM), w.dtype),
                             wd[:, :W_LORA], ad[:, :A_LORA], wd[:, W_LORA:], ad[:, A_LORA:],
                             part(6)], axis=1)
    return (wq.reshape(K, Q_DIM).astype(BF16), small.astype(BF16), part(3).astype(BF16),
            part(7).astype(BF16))


def _layer_params(l, w_in, q_norm, w_kv_up, w_br_mla, w_br_rwkv, w_out, w_ff_in, w_ff_out, **small):
    zeros = jnp.zeros((N_DIR, W_LORA, R_DIM), F32)
    w_q, w_small, w_rkv, w_gate = _split_w_in(w_in[l])
    qg = q_norm[l] * (QK_HEAD ** -0.5)
    q_gain = jnp.concatenate([qg[:NOPE_DIM], qg[:NOPE_DIM], qg[NOPE_DIM:], qg[NOPE_DIM:]])
    p = {name: val[l] for name, val in small.items()}
    p.update({
        'w_q': w_q, 'w_small': w_small, 'w_rkv': w_rkv, 'w_gate': w_gate,
        'q_gain': jnp.tile(q_gain, 2).reshape(1, 2 * Q_PAIR),
        'w_kv_up': w_kv_up[l].astype(BF16),
        'w_up_pad': jnp.concatenate([small['w_up'][l], zeros], axis=1).astype(BF16),
        'a_up_pad': jnp.concatenate([zeros, small['a_up'][l]], axis=1).astype(BF16),
        'g_up_bf16': small['g_up'][l].astype(BF16),
        'w_br_mla': w_br_mla[l].astype(BF16), 'w_br_rwkv': w_br_rwkv[l].astype(BF16),
        'w_out': w_out[l].astype(BF16),
        'w_ff_in': w_ff_in[l].astype(BF16), 'w_ff_out': w_ff_out[l].astype(BF16),
    })
    return p


def kernel(x_prompt, x_sample, cache_mla_ckv, cache_mla_kr, state_rwkv, c, c_ctx,
           norm1, w_ada, b_ada, w_in, q_norm, kv_norm, w_kv_up, k_norm, conv_rkv,
           k_k, k_a, r_k, w0, w_up, a0, a_up, g_up, lnx_w, lnx_b,
           w_br_mla, w_br_rwkv, w_out, norm2, w_ff_in, w_ff_out):
    x_p, x_s = x_prompt, x_sample
    n_lat = c.shape[0]
    ckv_list, kr_list, st_list = [], [], []
    for l in range(DEPTH):
        p = _layer_params(l, w_in, q_norm, w_kv_up, w_br_mla, w_br_rwkv, w_out, w_ff_in, w_ff_out,
                          norm1=norm1, kv_norm=kv_norm, k_norm=k_norm, conv_rkv=conv_rkv, k_k=k_k,
                          k_a=k_a, r_k=r_k, w0=w0, w_up=w_up, a0=a0, a_up=a_up, g_up=g_up,
                          lnx_w=lnx_w, lnx_b=lnx_b, norm2=norm2)
        cond = jnp.concatenate([c, c_ctx[None]], axis=0)
        cond = jnp.pad(jax.nn.silu(cond), ((0, (-cond.shape[0]) % 16), (0, 0)))
        mod = _matmul(cond, w_ada[l], tm=16, tn=1024)[:n_lat + 1] + b_ada[l]
        x_p, (ckv_l, kr_l, st_l) = _trunk_layer(x_p, mod[n_lat:], p, None)
        ckv_list.append(ckv_l)
        kr_list.append(kr_l)
        st_list.append(st_l)
        x_s, _ = _trunk_layer(x_s, mod[:n_lat], p,
                              (cache_mla_ckv[:, l], cache_mla_kr[:, l], state_rwkv[:, l]))
    return (x_p, x_s, jnp.stack(ckv_list, axis=1), jnp.stack(kr_list, axis=1),
            jnp.stack(st_list, axis=1))
```

```python
import functools

import jax
import jax.numpy as jnp
from jax import lax
from jax.experimental import pallas as pl
from jax.experimental.pallas import tpu as pltpu

D_MODEL = 2048
DEPTH = 1
GRID_W = 64
H_MLA = 16
NOPE_DIM = 128
ROPE_DIM = 64
QK_HEAD = NOPE_DIM + ROPE_DIM
V_HEAD = 128
KV_RANK = 512
ROPE_THETA = 10000.0
AXIS_DIM = ROPE_DIM // 2
AXIS_PAIRS = AXIS_DIM // 2
MLA_WIDTH = H_MLA * V_HEAD
Q_DIM = H_MLA * QK_HEAD
HEAD_RWKV = 64
H_RWKV = D_MODEL // HEAD_RWKV
R_DIM = H_RWKV * HEAD_RWKV
W_LORA = 64
A_LORA = 64
G_LORA = 128
N_DIR = 2
LNX_EPS = 64e-5
D_FF = 4 * D_MODEL
EPS = 1e-6
IN_SIZES = (Q_DIM, KV_RANK, ROPE_DIM, 3 * R_DIM, N_DIR * W_LORA, N_DIR * A_LORA, G_LORA, 2 * D_MODEL)
IN_DIM = sum(IN_SIZES)

F32 = jnp.float32
BF16 = jnp.bfloat16
VMEM_LIMIT = 56 * 1024 * 1024
SCAN_CHUNK = 64
SCAN_PAIRS = 16


def _params(sem):
    return pltpu.CompilerParams(dimension_semantics=sem, vmem_limit_bytes=VMEM_LIMIT)


def _mm_kernel(a_ref, b_ref, o_ref):
    o_ref[...] = jnp.dot(a_ref[...].astype(BF16), b_ref[...].astype(BF16),
                         preferred_element_type=F32).astype(o_ref.dtype)


def _matmul(a, b, *, tm=512, tn=512, out_dtype=F32):
    M, K = a.shape
    N = b.shape[1]
    tm = min(tm, M)
    tn = min(tn, N)
    assert M % tm == 0 and N % tn == 0, (M, N, tm, tn)
    return pl.pallas_call(
        _mm_kernel,
        grid=(M // tm, N // tn),
        in_specs=[pl.BlockSpec((tm, K), lambda i, j: (i, 0)),
                  pl.BlockSpec((K, tn), lambda i, j: (0, j))],
        out_specs=pl.BlockSpec((tm, tn), lambda i, j: (i, j)),
        out_shape=jax.ShapeDtypeStruct((M, N), out_dtype),
        compiler_params=_params(("parallel", "arbitrary")),
        name="matmul",
    )(a, b)


def _modulated_norm(x, g, scale, shift):
    ms = jnp.mean(x * x, axis=-1, keepdims=True)
    return (x * lax.rsqrt(ms + EPS) * g) * (1.0 + scale) + shift


def _mod_index(nb, rows_per_batch, tm):
    if nb == 1:
        return lambda i, j: (0, 0, 0)
    assert rows_per_batch % tm == 0
    return lambda i, j: (i * tm // rows_per_batch, 0, 0)


def _row_tile(tm, M, nb, rows_per_batch):
    tm = min(tm, M, rows_per_batch) if nb > 1 else min(tm, M)
    assert M % tm == 0
    return tm


def _sigmoid(x):
    return 1.0 / (1.0 + jnp.exp(-x))


def _swap16(x):
    lane = lax.broadcasted_iota(jnp.int32, x.shape, 1)
    return jnp.where((lane & AXIS_PAIRS) == 0,
                     pltpu.roll(x, 128 - AXIS_PAIRS, 1), pltpu.roll(x, AXIS_PAIRS, 1))


def _epi_plain(acc, extra, outs, rows):
    outs[0][rows, :] = acc.astype(outs[0].dtype)


def _epi_sigmoid(acc, extra, outs, rows):
    outs[0][rows, :] = _sigmoid(acc).astype(outs[0].dtype)


SMALL_LORA = KV_RANK + 128
SMALL_COLS = SMALL_LORA + 3 * 128


def _epi_small(acc, extra, outs, rows):
    ckv_ref, kr_ref, lora_ref = outs
    ckv_ref[rows, :] = acc[:, :KV_RANK]
    kr_ref[rows, :] = acc[:, KV_RANK:KV_RANK + ROPE_DIM]
    o = SMALL_LORA
    wa = acc[:, o:o + 256]
    lane = lax.broadcasted_iota(jnp.int32, wa.shape, 1)
    wa = jnp.where((lane & W_LORA) == 0, jnp.tanh(wa), wa)
    lora_ref[rows, :] = jnp.concatenate([wa, _sigmoid(acc[:, o + 256:o + 384])], axis=1)


Q_PAIR = 2 * QK_HEAD


def _epi_q(acc, extra, outs, rows, *, rope):
    g_ref = extra[0]
    tm, tn = acc.shape
    lo = lax.broadcasted_iota(jnp.int32, (tm, 128), 1) < ROPE_DIM
    parts = []
    for p in range(tn // Q_PAIR):
        z = acc[:, p * Q_PAIR:(p + 1) * Q_PAIR]
        g = g_ref[:, p * Q_PAIR:(p + 1) * Q_PAIR]
        n0, n1, rp = z[:, :128], z[:, 128:256], z[:, 256:]
        rp2 = rp * rp
        s0 = (jnp.sum(n0 * n0, axis=-1, keepdims=True)
              + jnp.sum(jnp.where(lo, rp2, 0.0), axis=-1, keepdims=True))
        s1 = (jnp.sum(n1 * n1, axis=-1, keepdims=True)
              + jnp.sum(jnp.where(lo, 0.0, rp2), axis=-1, keepdims=True))
        r0 = lax.rsqrt(s0 * (1.0 / QK_HEAD) + EPS)
        r1 = lax.rsqrt(s1 * (1.0 / QK_HEAD) + EPS)
        qr = rp * jnp.where(lo, r0, r1) * g[:, 256:]
        if rope:
            qr = qr * extra[1][rows, :] + _swap16(qr) * extra[2][rows, :]
        parts += [n0 * r0 * g[:, :128], n1 * r1 * g[:, 128:256], qr]
    outs[0][rows, :] = jnp.concatenate(parts, axis=1).astype(outs[0].dtype)


def _norm_mm_kernel(*refs, n_extra, n_out, epilogue, row_split):
    x_ref, g_ref, sc_ref, sh_ref, w_ref = refs[:5]
    extra = refs[5:5 + n_extra]
    outs = refs[5 + n_extra:5 + n_extra + n_out]
    h_scr = refs[-1]

    @pl.when(pl.program_id(1) == 0)
    def _():
        h = _modulated_norm(x_ref[...], g_ref[...], sc_ref[0], sh_ref[0])
        h_scr[...] = h.astype(BF16)

    step = h_scr.shape[0] // row_split
    for r in range(row_split):
        rows = pl.ds(r * step, step)
        epilogue(jnp.dot(h_scr[rows, :], w_ref[...], preferred_element_type=F32), extra, outs, rows)


def _norm_matmul(x, g, scale, shift, w, rows_per_batch, *, tm, tn, epilogue, outs,
                 extra=(), extra_specs=(), row_split=2, name):
    M, K = x.shape
    N = w.shape[1]
    nb = scale.shape[0]
    tm = _row_tile(tm, M, nb, rows_per_batch)
    assert N % tn == 0 and tm % (16 * row_split) == 0
    midx = _mod_index(nb, rows_per_batch, tm)
    res = pl.pallas_call(
        functools.partial(_norm_mm_kernel, n_extra=len(extra), n_out=len(outs), epilogue=epilogue,
                          row_split=row_split),
        grid=(M // tm, N // tn),
        in_specs=[pl.BlockSpec((tm, K), lambda i, j: (i, 0)),
                  pl.BlockSpec((1, K), lambda i, j: (0, 0)),
                  pl.BlockSpec((1, 1, K), midx),
                  pl.BlockSpec((1, 1, K), midx),
                  pl.BlockSpec((K, tn), lambda i, j: (0, j))] + [s(tm) for s in extra_specs],
        out_specs=[pl.BlockSpec((tm, c), lambda i, j: (i, j)) for c, _, _ in outs],
        out_shape=[jax.ShapeDtypeStruct((M, n), dt) for _, n, dt in outs],
        scratch_shapes=[pltpu.VMEM((tm, K), BF16)],
        compiler_params=_params(("parallel", "arbitrary")),
        name=name,
    )(x, g.reshape(1, K), scale, shift, w, *extra)
    return res


K_WIDTH = 2 * NOPE_DIM
KV_ROW_SPLIT = 4


def _kv_kernel(*refs, rope):
    if rope:
        (ckv_ref, kr_ref, gkv_ref, gk_ref, cos_ref, sin_ref, w_ref, k_ref, v_ref,
         a_scr, kr_scr, ss_scr) = refs
    else:
        ckv_ref, kr_ref, gkv_ref, gk_ref, w_ref, k_ref, v_ref, a_scr, kr_scr, ss_scr = refs
    h = pl.program_id(1)

    @pl.when(h == 0)
    def _():
        c = ckv_ref[...]
        a = c * lax.rsqrt(jnp.mean(c * c, axis=-1, keepdims=True) + EPS) * gkv_ref[...]
        a_scr[...] = a.astype(BF16)
        kr = kr_ref[...]
        ss_scr[...] = jnp.broadcast_to(0.5 * jnp.sum(kr * kr, axis=-1, keepdims=True), ss_scr.shape)
        krg = kr * gk_ref[:, NOPE_DIM:]
        if rope:
            krg = krg * cos_ref[...] + _swap16(krg) * sin_ref[...]
        kr_scr[...] = krg

    step = a_scr.shape[0] // KV_ROW_SPLIT
    for r in range(KV_ROW_SPLIT):
        rows = pl.ds(r * step, step)
        acc = jnp.dot(a_scr[rows, :], w_ref[...], preferred_element_type=F32)
        kn = acc[:, :NOPE_DIM]
        ssq = jnp.sum(kn * kn, axis=-1, keepdims=True) + ss_scr[rows, :1]
        rstd = lax.rsqrt(ssq * (1.0 / QK_HEAD) + EPS)
        lane = lax.broadcasted_iota(jnp.int32, kn.shape, 1)
        mine = (lane >> 6) == (h & 1)
        krh = jnp.where(mine, kr_scr[rows, :] * rstd, 0.0)
        k_ref[0, rows, :] = jnp.concatenate([kn * rstd * gk_ref[:, :NOPE_DIM], krh], axis=1).astype(BF16)
        v_ref[0, rows, :] = acc[:, NOPE_DIM:].astype(BF16)


def _keys_values(ckv, kr, p, tables, *, tm):
    Mk = ckv.shape[0]
    tm = min(tm, Mk)
    assert Mk % tm == 0 and tm % (16 * KV_ROW_SPLIT) == 0
    rope = tables is not None
    kr2 = jnp.concatenate([kr, kr], axis=-1)
    gk = jnp.concatenate([p['k_norm'], p['k_norm'][NOPE_DIM:]]).reshape(1, K_WIDTH)
    row = lambda i, h: (i, 0)
    fixed = lambda i, h: (0, 0)
    in_specs = [pl.BlockSpec((tm, KV_RANK), row), pl.BlockSpec((tm, 128), row),
                pl.BlockSpec((1, KV_RANK), fixed), pl.BlockSpec((1, K_WIDTH), fixed)]
    args = [ckv, kr2, p['kv_norm'].reshape(1, KV_RANK), gk]
    if rope:
        in_specs += [pl.BlockSpec((tm, 128), fixed), pl.BlockSpec((tm, 128), fixed)]
        args += list(tables)
    in_specs.append(pl.BlockSpec((KV_RANK, NOPE_DIM + V_HEAD), lambda i, h: (0, h)))
    args.append(p['w_kv_up'])
    return pl.pallas_call(
        functools.partial(_kv_kernel, rope=rope),
        grid=(Mk // tm, H_MLA),
        in_specs=in_specs,
        out_specs=[pl.BlockSpec((1, tm, K_WIDTH), lambda i, h: (h, i, 0)),
                   pl.BlockSpec((1, tm, V_HEAD), lambda i, h: (h, i, 0))],
        out_shape=[jax.ShapeDtypeStruct((H_MLA, Mk, K_WIDTH), BF16),
                   jax.ShapeDtypeStruct((H_MLA, Mk, V_HEAD), BF16)],
        scratch_shapes=[pltpu.VMEM((tm, KV_RANK), BF16), pltpu.VMEM((tm, 128), F32),
                        pltpu.VMEM((tm, 128), F32)],
        compiler_params=_params(("parallel", "arbitrary")),
        name="keys_values",
    )(*args)


def _attn_kernel(q_ref, k_ref, v_ref, o_ref, *, pairs):
    for pr in range(pairs):
        q = q_ref[0, :, pr * Q_PAIR:(pr + 1) * Q_PAIR]
        qr = q[:, 2 * NOPE_DIM:]
        for h in range(2):
            hh = 2 * pr + h
            qh = jnp.concatenate([q[:, h * NOPE_DIM:(h + 1) * NOPE_DIM], qr], axis=1)
            s = lax.dot_general(qh, k_ref[hh], (((1,), (1,)), ((), ())), preferred_element_type=F32)
            m = jnp.max(s, axis=-1, keepdims=True)
            p = jnp.exp(s - m)
            l = jnp.sum(p, axis=-1, keepdims=True)
            o = jnp.dot(p.astype(BF16), v_ref[hh], preferred_element_type=F32) / l
            o_ref[0, :, hh * V_HEAD:(hh + 1) * V_HEAD] = o.astype(o_ref.dtype)


def _attention(q, k, v, S, *, tq=256, pairs):
    B, T, _ = q.shape
    tq = min(tq, T)
    hp = 2 * pairs
    return pl.pallas_call(
        functools.partial(_attn_kernel, pairs=pairs),
        grid=(B, H_MLA // hp, T // tq),
        in_specs=[pl.BlockSpec((1, tq, pairs * Q_PAIR), lambda b, p, i: (b, i, p)),
                  pl.BlockSpec((hp, S, K_WIDTH), lambda b, p, i: (p, b, 0)),
                  pl.BlockSpec((hp, S, V_HEAD), lambda b, p, i: (p, b, 0))],
        out_specs=pl.BlockSpec((1, tq, hp * V_HEAD), lambda b, p, i: (b, i, p)),
        out_shape=jax.ShapeDtypeStruct((B, T, MLA_WIDTH), BF16),
        compiler_params=_params(("parallel", "parallel", "arbitrary")),
        name="attention",
    )(q, k, v)


def _branch_kernel(a_ref, b_ref, ga_ref, gb_ref, wa_ref, wb_ref, o_ref):
    ya = jnp.dot(a_ref[...], wa_ref[...], preferred_element_type=F32)
    yb = jnp.dot(b_ref[...], wb_ref[...], preferred_element_type=F32)
    o_ref[...] = (ga_ref[...].astype(F32) * ya + gb_ref[...].astype(F32) * yb).astype(o_ref.dtype)


def _branch_merge(a, b, gates, wa, wb, *, tm=1024, tn=1024):
    M, K = a.shape
    N = wa.shape[1]
    tm = min(tm, M)
    nj = N // tn
    return pl.pallas_call(
        _branch_kernel,
        grid=(M // tm, nj),
        in_specs=[pl.BlockSpec((tm, K), lambda i, j: (i, 0)),
                  pl.BlockSpec((tm, K), lambda i, j: (i, 0)),
                  pl.BlockSpec((tm, tn), lambda i, j: (i, j)),
                  pl.BlockSpec((tm, tn), lambda i, j: (i, j + nj)),
                  pl.BlockSpec((K, tn), lambda i, j: (0, j)),
                  pl.BlockSpec((K, tn), lambda i, j: (0, j))],
        out_specs=pl.BlockSpec((tm, tn), lambda i, j: (i, j)),
        out_shape=jax.ShapeDtypeStruct((M, N), BF16),
        compiler_params=_params(("parallel", "arbitrary")),
        name="branch_merge",
    )(a, b, gates, gates, wa, wb)


def _resid_kernel(m_ref, w_ref, x_ref, gt_ref, o_ref):
    o_ref[...] = x_ref[...] + gt_ref[0] * jnp.dot(m_ref[...], w_ref[...],
                                                  preferred_element_type=F32)


def _out_proj(m, w, x, gate, rows_per_batch, *, tm=1024, tn=1024):
    M, K = m.shape
    N = w.shape[1]
    nb = gate.shape[0]
    tm = _row_tile(tm, M, nb, rows_per_batch)
    if nb == 1:
        gidx = lambda i, j: (0, 0, j)
    else:
        gidx = lambda i, j: (i * tm // rows_per_batch, 0, j)
    return pl.pallas_call(
        _resid_kernel,
        grid=(M // tm, N // tn),
        in_specs=[pl.BlockSpec((tm, K), lambda i, j: (i, 0)),
                  pl.BlockSpec((K, tn), lambda i, j: (0, j)),
                  pl.BlockSpec((tm, tn), lambda i, j: (i, j)),
                  pl.BlockSpec((1, 1, tn), gidx)],
        out_specs=pl.BlockSpec((tm, tn), lambda i, j: (i, j)),
        out_shape=jax.ShapeDtypeStruct((M, N), F32),
        compiler_params=_params(("parallel", "arbitrary")),
        name="out_proj",
    )(m, w, x, gate)


def _ffn_kernel(x_ref, g_ref, sc_ref, sh_ref, gt_ref, w1_ref, w2_ref, o_ref, h_scr, acc_scr):
    f = pl.program_id(1)

    @pl.when(f == 0)
    def _():
        h = _modulated_norm(x_ref[...], g_ref[...], sc_ref[0], sh_ref[0])
        h_scr[...] = h.astype(BF16)
        acc_scr[...] = jnp.zeros_like(acc_scr)

    u = jnp.dot(h_scr[...], w1_ref[...], preferred_element_type=F32)
    u = jnp.square(jnp.maximum(u, 0.0))
    acc_scr[...] += jnp.dot(u.astype(BF16), w2_ref[...], preferred_element_type=F32)

    @pl.when(f == pl.num_programs(1) - 1)
    def _():
        o_ref[...] = x_ref[...] + gt_ref[0] * acc_scr[...]


def _ffn(x, g, scale, shift, gate, w1, w2, rows_per_batch, *, tm=512, tf=1024):
    M, K = x.shape
    F = w1.shape[1]
    nb = scale.shape[0]
    tm = _row_tile(tm, M, nb, rows_per_batch)
    assert F % tf == 0
    midx = _mod_index(nb, rows_per_batch, tm)
    return pl.pallas_call(
        _ffn_kernel,
        grid=(M // tm, F // tf),
        in_specs=[pl.BlockSpec((tm, K), lambda i, j: (i, 0)),
                  pl.BlockSpec((1, K), lambda i, j: (0, 0)),
                  pl.BlockSpec((1, 1, K), midx),
                  pl.BlockSpec((1, 1, K), midx),
                  pl.BlockSpec((1, 1, K), midx),
                  pl.BlockSpec((K, tf), lambda i, j: (0, j)),
                  pl.BlockSpec((tf, K), lambda i, j: (j, 0))],
        out_specs=pl.BlockSpec((tm, K), lambda i, j: (i, 0)),
        out_shape=jax.ShapeDtypeStruct((M, K), F32),
        scratch_shapes=[pltpu.VMEM((tm, K), BF16), pltpu.VMEM((tm, K), F32)],
        compiler_params=_params(("parallel", "arbitrary")),
        name="ffn",
    )(x, g.reshape(1, K), scale, shift, gate, w1, w2)


def _bdot(a, b, dims=((1,), (0,))):
    return lax.dot_general(a.astype(BF16), b.astype(BF16), (dims, ((), ())),
                           preferred_element_type=F32)


_NT = ((1,), (1,))
_TN = ((0,), (0,))
PAIR = 2 * HEAD_RWKV


def _scan_kernel(z0_ref, zp0_ref, zn0_ref, lo0_ref, z1_ref, zp1_ref, zn1_ref, lo1_ref,
                 cw_ref, kkg_ref, ka_ref, rk_ref, w0_ref, a0_ref, wup_ref, aup_ref, *rest,
                 chunk, pairs, n_chunks, zero_state):
    s0_ref = None if zero_state else rest[0]
    y0_ref, y1_ref, bon0_ref, bon1_ref, sf_ref, s_scr = rest[0 if zero_state else 1:]
    c = pl.program_id(1)
    C = chunk
    N = HEAD_RWKV

    @pl.when(c == 0)
    def _():
        if s0_ref is None:
            s_scr[...] = jnp.zeros(s_scr.shape, F32)
        else:
            for d in range(N_DIR):
                for p in range(pairs):
                    s_scr[d, p] = jnp.concatenate([s0_ref[0, d, 2 * p], s0_ref[0, d, 2 * p + 1]],
                                                  axis=1)

    row = lax.broadcasted_iota(jnp.int32, (C, PAIR), 0)
    lane = lax.broadcasted_iota(jnp.int32, (C, PAIR), 1)
    col = lane & (N - 1)
    eye = row == col
    head0 = lane < N
    ti = lax.broadcasted_iota(jnp.int32, (C, C), 0)
    tj = lax.broadcasted_iota(jnp.int32, (C, C), 1)

    def bd(x):
        return jnp.concatenate([jnp.where(head0, x, 0.0), jnp.where(head0, 0.0, x)],
                               axis=0).astype(BF16)

    def diag_blocks(x):
        return jnp.where(head0, x[:N], x[N:])

    def head_sum(x):
        s0 = jnp.sum(jnp.where(head0, x, 0.0), axis=-1, keepdims=True)
        s1 = jnp.sum(jnp.where(head0, 0.0, x), axis=-1, keepdims=True)
        return jnp.where(head0, s0, s1)

    def direction(d, z_ref, zp_ref, zn_ref, lo_ref, y_ref, bon_ref):
        tt = c if d == 0 else n_chunks - 1 - c
        first = tt == 0
        last = tt == n_chunks - 1
        if d == 0:
            incl, strict, tri = row >= col, row > col, ti >= tj
        else:
            incl, strict, tri = row <= col, row < col, ti <= tj
        tri = jnp.where(tri, 1.0, 0.0).astype(BF16)
        lora = lo_ref[0].astype(BF16)

        def conv(off, sl):
            zs = slice(off + sl.start, off + sl.stop)
            zc = z_ref[0, :, zs]
            before = jnp.where(first, 0.0, zp_ref[0, 7:8, zs])
            after = jnp.where(last, 0.0, zn_ref[0, 0:1, zs])
            zm = jnp.where(row == 0, before, pltpu.roll(zc, 1, 0))
            zq = jnp.where(row == C - 1, after, pltpu.roll(zc, C - 1, 0))
            return zm * cw_ref[0:1, zs] + zc * cw_ref[1:2, zs] + zq * cw_ref[2:3, zs]

        def s_lora(st, sl, p):
            up = jnp.dot(lora, jnp.concatenate([wup_ref[d, :, sl], aup_ref[d, :, sl]], axis=1),
                         preferred_element_type=F32)
            u = w0_ref[d, :, sl] + up[:, :PAIR]
            softplus = jnp.maximum(-u, 0.0) + jnp.log(1.0 + jnp.exp(-jnp.abs(u)))
            st['ld'] = -jnp.exp(-softplus - 0.5)
            st['a'] = _sigmoid(a0_ref[d, :, sl] + up[:, PAIR:])

        def s_cum(st, sl, p):
            ld = st['ld']
            ld_hi = ld.astype(BF16)
            rem = ld - ld_hi.astype(F32)
            ld_mid = rem.astype(BF16)
            ld_lo = (rem - ld_mid.astype(F32)).astype(BF16)
            pieces = jnp.dot(tri, jnp.concatenate([ld_hi, ld_mid, ld_lo], axis=1),
                             preferred_element_type=F32)
            st['cum'] = pieces[:, :PAIR] + pieces[:, PAIR:2 * PAIR] + pieces[:, 2 * PAIR:]
            st['tot'] = jnp.sum(ld, axis=0, keepdims=True)

        def s_conv(st, sl, p):
            st['R'] = conv(0, sl)
            st['Kraw'] = conv(R_DIM, sl)
            st['V'] = conv(2 * R_DIM, sl)

        def s_keys(st, sl, p):
            kx = st['Kraw'] * kkg_ref[:, sl]
            kk = kx * lax.rsqrt(head_sum(kx * kx) + 1e-12)
            a = st.pop('a')
            st['K'] = st.pop('Kraw') * (1.0 + (a - 1.0) * ka_ref[:, sl])
            st['Bv'] = kk * a
            st['kk'] = kk
            bon_ref[0, :, sl] = head_sum(st['R'] * st['K'] * rk_ref[:, sl]) * st['V']

        def s_decay(st, sl, p):
            cum, ld, tot = st.pop('cum'), st.pop('ld'), st['tot']
            e_neg = jnp.exp(-cum)
            e_end = jnp.exp(tot - cum)
            Bv, K = st.pop('Bv'), st.pop('K')
            st['At'] = -st.pop('kk') * jnp.exp(cum - ld)
            st['Rt'] = st.pop('R') * jnp.exp(cum)
            st['b_d'] = Bv * e_end
            st['k_d'] = K * e_end
            st['Bt'] = Bv * e_neg
            st['Kt'] = K * e_neg

        def s_scores(st, sl, p):
            st['bdv'] = bd(st['V'])
            st['sc'] = _bdot(jnp.concatenate([st['At'], st['Rt']], axis=0),
                             jnp.concatenate([bd(st.pop('Bt')), bd(st.pop('Kt'))], axis=0),
                             _NT)

        def s_masks(st, sl, p):
            sc = st.pop('sc')
            L = jnp.where(strict, sc[:C, :PAIR], 0.0)
            st['p_rb'] = jnp.where(incl, sc[C:, :PAIR], 0.0)
            st['p_rk'] = jnp.where(incl, sc[C:, PAIR:], 0.0)
            st['akv'] = _bdot(jnp.where(strict, sc[:C, PAIR:], 0.0), st['bdv'])
            st['x'] = jnp.where(eye, 1.0, 0.0) + L
            st['lp'] = _bdot(L, bd(L))

        def s_double(st, sl, p):
            o = _bdot(st['lp'], jnp.concatenate([bd(st['x']), bd(st['lp'])], axis=1))
            st['x'] = st['x'] + o[:, :PAIR]
            st['lp'] = o[:, PAIR:]

        def s_double_last(st, sl, p):
            st['x'] = st['x'] + _bdot(st.pop('lp'), bd(st['x']))

        def s_solve(st, sl, p):
            z = _bdot(st.pop('x'), jnp.concatenate([bd(st.pop('At')), bd(st.pop('akv'))], axis=1))
            st['a_p'], st['u_loc'] = z[:, :PAIR], z[:, PAIR:]

        def s_local(st, sl, p):
            a_p, u_loc = st.pop('a_p'), st.pop('u_loc')
            f = _bdot(st.pop('p_rb'), jnp.concatenate([bd(a_p), bd(u_loc)], axis=1))
            st['r_p'] = st.pop('Rt') + f[:, :PAIR]
            st['y_loc'] = f[:, PAIR:] + _bdot(st.pop('p_rk'), st.pop('bdv'))
            st['g'] = diag_blocks(_bdot(a_p, st['b_d'], _TN))
            st['s_loc'] = diag_blocks(_bdot(jnp.concatenate([u_loc, st.pop('V')], axis=0),
                                            jnp.concatenate([st.pop('b_d'), st.pop('k_d')], axis=0),
                                            _TN))

        def s_state(st, sl, p):
            S = s_scr[d, p]
            y_ref[0, :, sl] = _bdot(st.pop('r_p'), bd(S), _NT) + st.pop('y_loc')
            s_scr[d, p] = S * jnp.exp(st.pop('tot')) + _bdot(S, bd(st.pop('g'))) + st.pop('s_loc')

        prep = [s_lora, s_cum, s_conv, s_keys, s_decay]
        matrix = [s_scores, s_masks]
        n = 2
        while n < C:
            matrix.append(s_double if 2 * n < C else s_double_last)
            n *= 2
        matrix += [s_solve, s_local, s_state]
        return prep, matrix

    sls = [slice(p * PAIR, (p + 1) * PAIR) for p in range(pairs)]
    prep0, mat0 = direction(0, z0_ref, zp0_ref, zn0_ref, lo0_ref, y0_ref, bon0_ref)
    prep1, mat1 = direction(1, z1_ref, zp1_ref, zn1_ref, lo1_ref, y1_ref, bon1_ref)
    st0 = [dict() for _ in range(pairs)]
    st1 = [dict() for _ in range(pairs)]

    for fn in prep0:
        for p in range(pairs):
            fn(st0[p], sls[p], p)
    prep1_calls = [(fn, p) for fn in prep1 for p in range(pairs)]
    per = -(-len(prep1_calls) // len(mat0))
    for k, fn in enumerate(mat0):
        for p in range(pairs):
            fn(st0[p], sls[p], p)
        for pf, p in prep1_calls[k * per:(k + 1) * per]:
            pf(st1[p], sls[p], p)
    for fn in mat1:
        for p in range(pairs):
            fn(st1[p], sls[p], p)

    @pl.when(c == n_chunks - 1)
    def _():
        for d in range(N_DIR):
            for p in range(pairs):
                sf_ref[0, d, 2 * p] = s_scr[d, p, :, :N]
                sf_ref[0, d, 2 * p + 1] = s_scr[d, p, :, N:]


def _rwkv_scan(z_rkv, lora, p, s0):
    B, T, _ = z_rkv.shape
    C = SCAN_CHUNK
    nC = T // C
    n_pairs = H_RWKV // 2
    N = HEAD_RWKV
    hb = C // 8

    fwd = lambda cc: cc
    bwd = lambda cc: nC - 1 - cc

    def chunk_specs(tm, dd):
        return [
            pl.BlockSpec((1, C, 3 * R_DIM), lambda bb, cc: (bb, tm(cc), 0)),
            pl.BlockSpec((1, 8, 3 * R_DIM), lambda bb, cc: (bb, jnp.maximum(tm(cc) * hb - 1, 0), 0)),
            pl.BlockSpec((1, 8, 3 * R_DIM),
                         lambda bb, cc: (bb, jnp.minimum((tm(cc) + 1) * hb, T // 8 - 1), 0)),
            pl.BlockSpec((1, C, 128), lambda bb, cc: (bb, tm(cc), dd))]

    row3 = lambda a: a.reshape(1, -1)
    whole = lambda shape: pl.BlockSpec(shape, lambda bb, cc: (0,) * len(shape))
    vec = whole((1, R_DIM))
    out0 = pl.BlockSpec((1, C, R_DIM), lambda bb, cc: (bb, fwd(cc), 0))
    out1 = pl.BlockSpec((1, C, R_DIM), lambda bb, cc: (bb, bwd(cc), 0))
    state = pl.BlockSpec((1, N_DIR, H_RWKV, N, N), lambda bb, cc: (bb, 0, 0, 0, 0))
    in_specs = (chunk_specs(fwd, 0) + chunk_specs(bwd, 1)
                + [whole((3, 3 * R_DIM)), vec, vec, vec, whole((N_DIR, 1, R_DIM)), whole((N_DIR, 1, R_DIM)),
                   whole((N_DIR, 128, R_DIM)), whole((N_DIR, 128, R_DIM))]
                + ([] if s0 is None else [state]))
    ydir = jax.ShapeDtypeStruct((B, T, R_DIM), F32)
    y0, y1, b0, b1, s_fin = pl.pallas_call(
        functools.partial(_scan_kernel, chunk=C, pairs=n_pairs, n_chunks=nC, zero_state=s0 is None),
        grid=(B, nC),
        in_specs=in_specs,
        out_specs=[out0, out1, out0, out1, state],
        out_shape=[ydir, ydir, ydir, ydir, jax.ShapeDtypeStruct((B, N_DIR, H_RWKV, N, N), F32)],
        scratch_shapes=[pltpu.VMEM((N_DIR, n_pairs, N, PAIR), F32)],
        compiler_params=_params(("parallel", "arbitrary")),
        name="rwkv_scan",
    )(z_rkv, z_rkv, z_rkv, lora, z_rkv, z_rkv, z_rkv, lora, p['conv_rkv'], row3(p['k_k']),
      row3(p['k_a']), row3(p['r_k']), p['w0'].reshape(N_DIR, 1, R_DIM), p['a0'].reshape(N_DIR, 1, R_DIM),
      p['w_up_pad'], p['a_up_pad'], *([] if s0 is None else [s0]))
    return (y0, y1, b0, b1), s_fin


def _post_kernel(y0_ref, y1_ref, b0_ref, b1_ref, gd_ref, gup_ref, lw_ref, lb_ref, o_ref):
    tm = o_ref.shape[0]
    g = jnp.dot(gd_ref[...].astype(BF16), gup_ref[...], preferred_element_type=F32)
    head0 = lax.broadcasted_iota(jnp.int32, (tm, PAIR), 1) < HEAD_RWKV

    def head_mean(x):
        s0 = jnp.sum(jnp.where(head0, x, 0.0), axis=-1, keepdims=True)
        s1 = jnp.sum(jnp.where(head0, 0.0, x), axis=-1, keepdims=True)
        return jnp.where(head0, s0, s1) * (1.0 / HEAD_RWKV)

    for p in range(R_DIM // PAIR):
        sl = slice(p * PAIR, (p + 1) * PAIR)
        ys = y0_ref[:, sl] + y1_ref[:, sl]
        dev = ys - head_mean(ys)
        yn = dev * lax.rsqrt(head_mean(dev * dev) + LNX_EPS)
        o = yn * lw_ref[:, sl] + lb_ref[:, sl] + (b0_ref[:, sl] + b1_ref[:, sl])
        o_ref[:, sl] = (o * g[:, sl]).astype(o_ref.dtype)


def _rwkv_post(y0, y1, b0, b1, lora, p, *, tm=512):
    M, Rd = y0.shape
    tm = min(tm, M)
    assert M % tm == 0
    rows = pl.BlockSpec((tm, Rd), lambda i: (i, 0))
    vec = pl.BlockSpec((1, Rd), lambda i: (0, 0))
    return pl.pallas_call(
        _post_kernel,
        grid=(M // tm,),
        in_specs=[rows, rows, rows, rows, pl.BlockSpec((tm, 128), lambda i: (i, 2)),
                  pl.BlockSpec((G_LORA, Rd), lambda i: (0, 0)), vec, vec],
        out_specs=pl.BlockSpec((tm, Rd), lambda i: (i, 0)),
        out_shape=jax.ShapeDtypeStruct((M, Rd), BF16),
        compiler_params=_params(("parallel",)),
        name="rwkv_post",
    )(y0, y1, b0, b1, lora, p['g_up_bf16'], p['lnx_w'].reshape(1, Rd), p['lnx_b'].reshape(1, Rd))


def _rope_tables(T, n_cache):
    rows = T // GRID_W
    row = jnp.repeat(jnp.arange(rows, dtype=F32), GRID_W)
    col = jnp.tile(jnp.arange(GRID_W, dtype=F32), rows)
    inv_freq = jnp.power(ROPE_THETA, -jnp.arange(AXIS_PAIRS, dtype=F32) / AXIS_PAIRS)
    ar, ac = row[:, None] * inv_freq, col[:, None] * inv_freq
    cos = jnp.concatenate([jnp.cos(ar), jnp.cos(ar), jnp.cos(ac), jnp.cos(ac)], axis=1)
    sin = jnp.concatenate([-jnp.sin(ar), jnp.sin(ar), -jnp.sin(ac), jnp.sin(ac)], axis=1)
    cos = jnp.concatenate([cos, jnp.ones((n_cache, ROPE_DIM), F32)], axis=0)
    sin = jnp.concatenate([sin, jnp.zeros((n_cache, ROPE_DIM), F32)], axis=0)
    return jnp.tile(cos, (1, 2)), jnp.tile(sin, (1, 2))


def _trunk_layer(x, mod, p, cache):
    B, T, _ = x.shape
    M = B * T
    nb = mod.shape[0]
    shift1, scale1, gate1, shift2, scale2, gate2 = [
        m.reshape(nb, 1, D_MODEL) for m in jnp.split(mod, 6, axis=-1)]
    xf = x.reshape(M, D_MODEL)
    latent = cache is not None
    proj = functools.partial(_norm_matmul, xf, p['norm1'], scale1, shift1, rows_per_batch=T)

    q_tm = 512
    extra, extra_specs = [p['q_gain']], [lambda tm: pl.BlockSpec((1, 2 * Q_PAIR), lambda i, j: (0, 0))]
    if latent:
        cos, sin = _rope_tables(T, cache[0].shape[1])
        n_t = T // min(q_tm, T)
        tbl = lambda tm: pl.BlockSpec((tm, 128), lambda i, j: (i % n_t, 0))
        extra += [cos[:T], sin[:T]]
        extra_specs += [tbl, tbl]
    q, = proj(p['w_q'], tm=q_tm, tn=2 * Q_PAIR, epilogue=functools.partial(_epi_q, rope=latent),
              outs=[(2 * Q_PAIR, Q_DIM, BF16)], extra=extra, extra_specs=extra_specs, name="proj_q")
    z_ckv, z_kr, lora = proj(p['w_small'], tm=1024, tn=SMALL_COLS, epilogue=_epi_small,
                             outs=[(KV_RANK, KV_RANK, F32), (ROPE_DIM, ROPE_DIM, F32), (384, 384, F32)],
                             name="proj_small")
    z_rkv, = proj(p['w_rkv'], tm=1024, tn=1024, epilogue=_epi_plain,
                  outs=[(1024, 3 * R_DIM, F32)], name="proj_rkv")
    gates, = proj(p['w_gate'], tm=1024, tn=1024, epilogue=_epi_sigmoid,
                  outs=[(1024, 2 * D_MODEL, BF16)], name="proj_gate")

    if latent:
        ckv_ctx, kr_ctx, s0 = cache
        S = T + ckv_ctx.shape[1]
        ckv_all = jnp.concatenate([z_ckv.reshape(B, T, KV_RANK), ckv_ctx], axis=1).reshape(B * S, KV_RANK)
        kr_all = jnp.concatenate([z_kr.reshape(B, T, ROPE_DIM), kr_ctx], axis=1).reshape(B * S, ROPE_DIM)
        k, v = _keys_values(ckv_all, kr_all, p, (cos, sin), tm=S)
    else:
        S = T
        s0 = None
        k, v = _keys_values(z_ckv, z_kr, p, None, tm=1024)
    o_mla = _attention(q.reshape(B, T, Q_DIM), k, v, S, pairs=4 if latent else 8).reshape(M, MLA_WIDTH)

    yb, s_final = _rwkv_scan(z_rkv.reshape(B, T, 3 * R_DIM), lora.reshape(B, T, 384), p, s0)
    o_rwkv = _rwkv_post(*[a.reshape(M, R_DIM) for a in yb], lora, p)

    merged = _branch_merge(o_mla, o_rwkv, gates, p['w_br_mla'], p['w_br_rwkv'])
    x1 = _out_proj(merged, p['w_out'], xf, gate1, T)
    x2 = _ffn(x1, p['norm2'], scale2, shift2, gate2, p['w_ff_in'], p['w_ff_out'], T)
    return (x2.reshape(B, T, D_MODEL),
            (z_ckv.reshape(B, T, KV_RANK), z_kr.reshape(B, T, ROPE_DIM), s_final))


def _split_w_in(w):
    offs = [sum(IN_SIZES[:i]) for i in range(len(IN_SIZES))]
    part = lambda i: w[:, offs[i]:offs[i] + IN_SIZES[i]]
    K = w.shape[0]
    wq = part(0).reshape(K, H_MLA // 2, 2, QK_HEAD)
    wq = jnp.concatenate([wq[..., :NOPE_DIM].reshape(K, H_MLA // 2, 2 * NOPE_DIM),
                          wq[..., NOPE_DIM:].reshape(K, H_MLA // 2, 2 * ROPE_DIM)], axis=-1)
    wd, ad = part(4), part(5)
    small = jnp.concatenate([part(1), part(2), jnp.zeros((K, 128 - ROPE_DIM), w.dtype),
                             wd[:, :W_LORA], ad[:, :A_LORA], wd[:, W_LORA:], ad[:, A_LORA:],
                             part(6)], axis=1)
    return (wq.reshape(K, Q_DIM).astype(BF16), small.astype(BF16), part(3).astype(BF16),
            part(7).astype(BF16))


def _layer_params(l, w_in, q_norm, w_kv_up, w_br_mla, w_br_rwkv, w_out, w_ff_in, w_ff_out, **small):
    zeros = jnp.zeros((N_DIR, W_LORA, R_DIM), F32)
    w_q, w_small, w_rkv, w_gate = _split_w_in(w_in[l])
    qg = q_norm[l] * (QK_HEAD ** -0.5)
    q_gain = jnp.concatenate([qg[:NOPE_DIM], qg[:NOPE_DIM], qg[NOPE_DIM:], qg[NOPE_DIM:]])
    p = {name: val[l] for name, val in small.items()}
    p.update({
        'w_q': w_q, 'w_small': w_small, 'w_rkv': w_rkv, 'w_gate': w_gate,
        'q_gain': jnp.tile(q_gain, 2).reshape(1, 2 * Q_PAIR),
        'w_kv_up': w_kv_up[l].astype(BF16),
        'w_up_pad': jnp.concatenate([small['w_up'][l], zeros], axis=1).astype(BF16),
        'a_up_pad': jnp.concatenate([zeros, small['a_up'][l]], axis=1).astype(BF16),
        'g_up_bf16': small['g_up'][l].astype(BF16),
        'w_br_mla': w_br_mla[l].astype(BF16), 'w_br_rwkv': w_br_rwkv[l].astype(BF16),
        'w_out': w_out[l].astype(BF16),
        'w_ff_in': w_ff_in[l].astype(BF16), 'w_ff_out': w_ff_out[l].astype(BF16),
    })
    return p


def kernel(x_prompt, x_sample, cache_mla_ckv, cache_mla_kr, state_rwkv, c, c_ctx,
           norm1, w_ada, b_ada, w_in, q_norm, kv_norm, w_kv_up, k_norm, conv_rkv,
           k_k, k_a, r_k, w0, w_up, a0, a_up, g_up, lnx_w, lnx_b,
           w_br_mla, w_br_rwkv, w_out, norm2, w_ff_in, w_ff_out):
    x_p, x_s = x_prompt, x_sample
    n_lat = c.shape[0]
    ckv_list, kr_list, st_list = [], [], []
    for l in range(DEPTH):
        p = _layer_params(l, w_in, q_norm, w_kv_up, w_br_mla, w_br_rwkv, w_out, w_ff_in, w_ff_out,
                          norm1=norm1, kv_norm=kv_norm, k_norm=k_norm, conv_rkv=conv_rkv, k_k=k_k,
                          k_a=k_a, r_k=r_k, w0=w0, w_up=w_up, a0=a0, a_up=a_up, g_up=g_up,
                          lnx_w=lnx_w, lnx_b=lnx_b, norm2=norm2)
        cond = jnp.concatenate([c, c_ctx[None]], axis=0)
        cond = jnp.pad(jax.nn.silu(cond), ((0, (-cond.shape[0]) % 16), (0, 0)))
        mod = _matmul(cond, w_ada[l], tm=16, tn=1024)[:n_lat + 1] + b_ada[l]
        x_p, (ckv_l, kr_l, st_l) = _trunk_layer(x_p, mod[n_lat:], p, None)
        ckv_list.append(ckv_l)
        kr_list.append(kr_l)
        st_list.append(st_l)
        x_s, _ = _trunk_layer(x_s, mod[:n_lat], p,
                              (cache_mla_ckv[:, l], cache_mla_kr[:, l], state_rwkv[:, l]))
    return (x_p, x_s, jnp.stack(ckv_list, axis=1), jnp.stack(kr_list, axis=1),
            jnp.stack(st_list, axis=1))
```

```python
import functools

import jax
import jax.numpy as jnp
from jax import lax
from jax.experimental import pallas as pl
from jax.experimental.pallas import tpu as pltpu

D_MODEL = 2048
DEPTH = 1
GRID_W = 64
H_MLA = 16
NOPE_DIM = 128
ROPE_DIM = 64
QK_HEAD = NOPE_DIM + ROPE_DIM
V_HEAD = 128
KV_RANK = 512
ROPE_THETA = 10000.0
AXIS_DIM = ROPE_DIM // 2
AXIS_PAIRS = AXIS_DIM // 2
MLA_WIDTH = H_MLA * V_HEAD
Q_DIM = H_MLA * QK_HEAD
HEAD_RWKV = 64
H_RWKV = D_MODEL // HEAD_RWKV
R_DIM = H_RWKV * HEAD_RWKV
W_LORA = 64
A_LORA = 64
G_LORA = 128
N_DIR = 2
LNX_EPS = 64e-5
D_FF = 4 * D_MODEL
EPS = 1e-6
IN_SIZES = (Q_DIM, KV_RANK, ROPE_DIM, 3 * R_DIM, N_DIR * W_LORA, N_DIR * A_LORA, G_LORA, 2 * D_MODEL)
IN_DIM = sum(IN_SIZES)

F32 = jnp.float32
BF16 = jnp.bfloat16
VMEM_LIMIT = 56 * 1024 * 1024
SCAN_CHUNK = 64
SCAN_PAIRS = 16


def _params(sem):
    return pltpu.CompilerParams(dimension_semantics=sem, vmem_limit_bytes=VMEM_LIMIT)


def _mm_kernel(a_ref, b_ref, o_ref):
    o_ref[...] = jnp.dot(a_ref[...].astype(BF16), b_ref[...].astype(BF16),
                         preferred_element_type=F32).astype(o_ref.dtype)


def _matmul(a, b, *, tm=512, tn=512, out_dtype=F32):
    M, K = a.shape
    N = b.shape[1]
    tm = min(tm, M)
    tn = min(tn, N)
    assert M % tm == 0 and N % tn == 0, (M, N, tm, tn)
    return pl.pallas_call(
        _mm_kernel,
        grid=(M // tm, N // tn),
        in_specs=[pl.BlockSpec((tm, K), lambda i, j: (i, 0)),
                  pl.BlockSpec((K, tn), lambda i, j: (0, j))],
        out_specs=pl.BlockSpec((tm, tn), lambda i, j: (i, j)),
        out_shape=jax.ShapeDtypeStruct((M, N), out_dtype),
        compiler_params=_params(("parallel", "arbitrary")),
        name="matmul",
    )(a, b)


def _modulated_norm(x, g, scale, shift):
    ms = jnp.mean(x * x, axis=-1, keepdims=True)
    return (x * lax.rsqrt(ms + EPS) * g) * (1.0 + scale) + shift


def _mod_index(nb, rows_per_batch, tm):
    if nb == 1:
        return lambda i, j: (0, 0, 0)
    assert rows_per_batch % tm == 0
    return lambda i, j: (i * tm // rows_per_batch, 0, 0)


def _row_tile(tm, M, nb, rows_per_batch):
    tm = min(tm, M, rows_per_batch) if nb > 1 else min(tm, M)
    assert M % tm == 0
    return tm


def _sigmoid(x):
    return 1.0 / (1.0 + jnp.exp(-x))


def _swap16(x):
    lane = lax.broadcasted_iota(jnp.int32, x.shape, 1)
    return jnp.where((lane & AXIS_PAIRS) == 0,
                     pltpu.roll(x, 128 - AXIS_PAIRS, 1), pltpu.roll(x, AXIS_PAIRS, 1))


def _epi_plain(acc, extra, outs, rows):
    outs[0][rows, :] = acc.astype(outs[0].dtype)


def _epi_sigmoid(acc, extra, outs, rows):
    outs[0][rows, :] = _sigmoid(acc).astype(outs[0].dtype)


SMALL_LORA = KV_RANK + 128
SMALL_COLS = SMALL_LORA + 3 * 128


def _epi_small(acc, extra, outs, rows):
    ckv_ref, kr_ref, lora_ref = outs
    ckv_ref[rows, :] = acc[:, :KV_RANK]
    kr_ref[rows, :] = acc[:, KV_RANK:KV_RANK + ROPE_DIM]
    o = SMALL_LORA
    wa = acc[:, o:o + 256]
    lane = lax.broadcasted_iota(jnp.int32, wa.shape, 1)
    wa = jnp.where((lane & W_LORA) == 0, jnp.tanh(wa), wa)
    lora_ref[rows, :] = jnp.concatenate([wa, _sigmoid(acc[:, o + 256:o + 384])], axis=1)


Q_PAIR = 2 * QK_HEAD


def _epi_q(acc, extra, outs, rows, *, rope):
    g_ref = extra[0]
    tm, tn = acc.shape
    lo = lax.broadcasted_iota(jnp.int32, (tm, 128), 1) < ROPE_DIM
    parts = []
    for p in range(tn // Q_PAIR):
        z = acc[:, p * Q_PAIR:(p + 1) * Q_PAIR]
        g = g_ref[:, p * Q_PAIR:(p + 1) * Q_PAIR]
        n0, n1, rp = z[:, :128], z[:, 128:256], z[:, 256:]
        rp2 = rp * rp
        s0 = (jnp.sum(n0 * n0, axis=-1, keepdims=True)
              + jnp.sum(jnp.where(lo, rp2, 0.0), axis=-1, keepdims=True))
        s1 = (jnp.sum(n1 * n1, axis=-1, keepdims=True)
              + jnp.sum(jnp.where(lo, 0.0, rp2), axis=-1, keepdims=True))
        r0 = lax.rsqrt(s0 * (1.0 / QK_HEAD) + EPS)
        r1 = lax.rsqrt(s1 * (1.0 / QK_HEAD) + EPS)
        qr = rp * jnp.where(lo, r0, r1) * g[:, 256:]
        if rope:
            qr = qr * extra[1][rows, :] + _swap16(qr) * extra[2][rows, :]
        parts += [n0 * r0 * g[:, :128], n1 * r1 * g[:, 128:256], qr]
    outs[0][rows, :] = jnp.concatenate(parts, axis=1).astype(outs[0].dtype)


def _norm_mm_kernel(*refs, n_extra, n_out, epilogue, row_split):
    x_ref, g_ref, sc_ref, sh_ref, w_ref = refs[:5]
    extra = refs[5:5 + n_extra]
    outs = refs[5 + n_extra:5 + n_extra + n_out]
    h_scr = refs[-1]

    @pl.when(pl.program_id(1) == 0)
    def _():
        h = _modulated_norm(x_ref[...], g_ref[...], sc_ref[0], sh_ref[0])
        h_scr[...] = h.astype(BF16)

    step = h_scr.shape[0] // row_split
    for r in range(row_split):
        rows = pl.ds(r * step, step)
        epilogue(jnp.dot(h_scr[rows, :], w_ref[...], preferred_element_type=F32), extra, outs, rows)


def _norm_matmul(x, g, scale, shift, w, rows_per_batch, *, tm, tn, epilogue, outs,
                 extra=(), extra_specs=(), row_split=2, name):
    M, K = x.shape
    N = w.shape[1]
    nb = scale.shape[0]
    tm = _row_tile(tm, M, nb, rows_per_batch)
    assert N % tn == 0 and tm % (16 * row_split) == 0
    midx = _mod_index(nb, rows_per_batch, tm)
    res = pl.pallas_call(
        functools.partial(_norm_mm_kernel, n_extra=len(extra), n_out=len(outs), epilogue=epilogue,
                          row_split=row_split),
        grid=(M // tm, N // tn),
        in_specs=[pl.BlockSpec((tm, K), lambda i, j: (i, 0)),
                  pl.BlockSpec((1, K), lambda i, j: (0, 0)),
                  pl.BlockSpec((1, 1, K), midx),
                  pl.BlockSpec((1, 1, K), midx),
                  pl.BlockSpec((K, tn), lambda i, j: (0, j))] + [s(tm) for s in extra_specs],
        out_specs=[pl.BlockSpec((tm, c), lambda i, j: (i, j)) for c, _, _ in outs],
        out_shape=[jax.ShapeDtypeStruct((M, n), dt) for _, n, dt in outs],
        scratch_shapes=[pltpu.VMEM((tm, K), BF16)],
        compiler_params=_params(("parallel", "arbitrary")),
        name=name,
    )(x, g.reshape(1, K), scale, shift, w, *extra)
    return res


K_WIDTH = 2 * NOPE_DIM
KV_ROW_SPLIT = 4


def _kv_kernel(*refs, rope):
    if rope:
        (ckv_ref, kr_ref, gkv_ref, gk_ref, cos_ref, sin_ref, w_ref, k_ref, v_ref,
         a_scr, kr_scr, ss_scr) = refs
    else:
        ckv_ref, kr_ref, gkv_ref, gk_ref, w_ref, k_ref, v_ref, a_scr, kr_scr, ss_scr = refs
    h = pl.program_id(1)

    @pl.when(h == 0)
    def _():
        c = ckv_ref[...]
        a = c * lax.rsqrt(jnp.mean(c * c, axis=-1, keepdims=True) + EPS) * gkv_ref[...]
        a_scr[...] = a.astype(BF16)
        kr = kr_ref[...]
        ss_scr[...] = jnp.broadcast_to(0.5 * jnp.sum(kr * kr, axis=-1, keepdims=True), ss_scr.shape)
        krg = kr * gk_ref[:, NOPE_DIM:]
        if rope:
            krg = krg * cos_ref[...] + _swap16(krg) * sin_ref[...]
        kr_scr[...] = krg

    step = a_scr.shape[0] // KV_ROW_SPLIT
    for r in range(KV_ROW_SPLIT):
        rows = pl.ds(r * step, step)
        acc = jnp.dot(a_scr[rows, :], w_ref[...], preferred_element_type=F32)
        kn = acc[:, :NOPE_DIM]
        ssq = jnp.sum(kn * kn, axis=-1, keepdims=True) + ss_scr[rows, :1]
        rstd = lax.rsqrt(ssq * (1.0 / QK_HEAD) + EPS)
        lane = lax.broadcasted_iota(jnp.int32, kn.shape, 1)
        mine = (lane >> 6) == (h & 1)
        krh = jnp.where(mine, kr_scr[rows, :] * rstd, 0.0)
        k_ref[0, rows, :] = jnp.concatenate([kn * rstd * gk_ref[:, :NOPE_DIM], krh], axis=1).astype(BF16)
        v_ref[0, rows, :] = acc[:, NOPE_DIM:].astype(BF16)


def _keys_values(ckv, kr, p, tables, *, tm):
    Mk = ckv.shape[0]
    tm = min(tm, Mk)
    assert Mk % tm == 0 and tm % (16 * KV_ROW_SPLIT) == 0
    rope = tables is not None
    kr2 = jnp.concatenate([kr, kr], axis=-1)
    gk = jnp.concatenate([p['k_norm'], p['k_norm'][NOPE_DIM:]]).reshape(1, K_WIDTH)
    row = lambda i, h: (i, 0)
    fixed = lambda i, h: (0, 0)
    in_specs = [pl.BlockSpec((tm, KV_RANK), row), pl.BlockSpec((tm, 128), row),
                pl.BlockSpec((1, KV_RANK), fixed), pl.BlockSpec((1, K_WIDTH), fixed)]
    args = [ckv, kr2, p['kv_norm'].reshape(1, KV_RANK), gk]
    if rope:
        in_specs += [pl.BlockSpec((tm, 128), fixed), pl.BlockSpec((tm, 128), fixed)]
        args += list(tables)
    in_specs.append(pl.BlockSpec((KV_RANK, NOPE_DIM + V_HEAD), lambda i, h: (0, h)))
    args.append(p['w_kv_up'])
    return pl.pallas_call(
        functools.partial(_kv_kernel, rope=rope),
        grid=(Mk // tm, H_MLA),
        in_specs=in_specs,
        out_specs=[pl.BlockSpec((1, tm, K_WIDTH), lambda i, h: (h, i, 0)),
                   pl.BlockSpec((1, tm, V_HEAD), lambda i, h: (h, i, 0))],
        out_shape=[jax.ShapeDtypeStruct((H_MLA, Mk, K_WIDTH), BF16),
                   jax.ShapeDtypeStruct((H_MLA, Mk, V_HEAD), BF16)],
        scratch_shapes=[pltpu.VMEM((tm, KV_RANK), BF16), pltpu.VMEM((tm, 128), F32),
                        pltpu.VMEM((tm, 128), F32)],
        compiler_params=_params(("parallel", "arbitrary")),
        name="keys_values",
    )(*args)


def _attn_kernel(q_ref, k_ref, v_ref, o_ref, *, pairs):
    for pr in range(pairs):
        q = q_ref[0, :, pr * Q_PAIR:(pr + 1) * Q_PAIR]
        qr = q[:, 2 * NOPE_DIM:]
        for h in range(2):
            hh = 2 * pr + h
            qh = jnp.concatenate([q[:, h * NOPE_DIM:(h + 1) * NOPE_DIM], qr], axis=1)
            s = lax.dot_general(qh, k_ref[hh], (((1,), (1,)), ((), ())), preferred_element_type=F32)
            m = jnp.max(s, axis=-1, keepdims=True)
            p = jnp.exp(s - m)
            l = jnp.sum(p, axis=-1, keepdims=True)
            o = jnp.dot(p.astype(BF16), v_ref[hh], preferred_element_type=F32) / l
            o_ref[0, :, hh * V_HEAD:(hh + 1) * V_HEAD] = o.astype(o_ref.dtype)


def _attention(q, k, v, S, *, tq=256, pairs):
    B, T, _ = q.shape
    tq = min(tq, T)
    hp = 2 * pairs
    return pl.pallas_call(
        functools.partial(_attn_kernel, pairs=pairs),
        grid=(B, H_MLA // hp, T // tq),
        in_specs=[pl.BlockSpec((1, tq, pairs * Q_PAIR), lambda b, p, i: (b, i, p)),
                  pl.BlockSpec((hp, S, K_WIDTH), lambda b, p, i: (p, b, 0)),
                  pl.BlockSpec((hp, S, V_HEAD), lambda b, p, i: (p, b, 0))],
        out_specs=pl.BlockSpec((1, tq, hp * V_HEAD), lambda b, p, i: (b, i, p)),
        out_shape=jax.ShapeDtypeStruct((B, T, MLA_WIDTH), BF16),
        compiler_params=_params(("parallel", "parallel", "arbitrary")),
        name="attention",
    )(q, k, v)


def _branch_kernel(a_ref, b_ref, ga_ref, gb_ref, wa_ref, wb_ref, o_ref):
    ya = jnp.dot(a_ref[...], wa_ref[...], preferred_element_type=F32)
    yb = jnp.dot(b_ref[...], wb_ref[...], preferred_element_type=F32)
    o_ref[...] = (ga_ref[...].astype(F32) * ya + gb_ref[...].astype(F32) * yb).astype(o_ref.dtype)


def _branch_merge(a, b, gates, wa, wb, *, tm=1024, tn=1024):
    M, K = a.shape
    N = wa.shape[1]
    tm = min(tm, M)
    nj = N // tn
    return pl.pallas_call(
        _branch_kernel,
        grid=(M // tm, nj),
        in_specs=[pl.BlockSpec((tm, K), lambda i, j: (i, 0)),
                  pl.BlockSpec((tm, K), lambda i, j: (i, 0)),
                  pl.BlockSpec((tm, tn), lambda i, j: (i, j)),
                  pl.BlockSpec((tm, tn), lambda i, j: (i, j + nj)),
                  pl.BlockSpec((K, tn), lambda i, j: (0, j)),
                  pl.BlockSpec((K, tn), lambda i, j: (0, j))],
        out_specs=pl.BlockSpec((tm, tn), lambda i, j: (i, j)),
        out_shape=jax.ShapeDtypeStruct((M, N), BF16),
        compiler_params=_params(("parallel", "arbitrary")),
        name="branch_merge",
    )(a, b, gates, gates, wa, wb)


def _resid_kernel(m_ref, w_ref, x_ref, gt_ref, o_ref):
    o_ref[...] = x_ref[...] + gt_ref[0] * jnp.dot(m_ref[...], w_ref[...],
                                                  preferred_element_type=F32)


def _out_proj(m, w, x, gate, rows_per_batch, *, tm=1024, tn=1024):
    M, K = m.shape
    N = w.shape[1]
    nb = gate.shape[0]
    tm = _row_tile(tm, M, nb, rows_per_batch)
    if nb == 1:
        gidx = lambda i, j: (0, 0, j)
    else:
        gidx = lambda i, j: (i * tm // rows_per_batch, 0, j)
    return pl.pallas_call(
        _resid_kernel,
        grid=(M // tm, N // tn),
        in_specs=[pl.BlockSpec((tm, K), lambda i, j: (i, 0)),
                  pl.BlockSpec((K, tn), lambda i, j: (0, j)),
                  pl.BlockSpec((tm, tn), lambda i, j: (i, j)),
                  pl.BlockSpec((1, 1, tn), gidx)],
        out_specs=pl.BlockSpec((tm, tn), lambda i, j: (i, j)),
        out_shape=jax.ShapeDtypeStruct((M, N), F32),
        compiler_params=_params(("parallel", "arbitrary")),
        name="out_proj",
    )(m, w, x, gate)


def _ffn_kernel(x_ref, g_ref, sc_ref, sh_ref, gt_ref, w1_ref, w2_ref, o_ref, h_scr, acc_scr):
    f = pl.program_id(1)

    @pl.when(f == 0)
    def _():
        h = _modulated_norm(x_ref[...], g_ref[...], sc_ref[0], sh_ref[0])
        h_scr[...] = h.astype(BF16)
        acc_scr[...] = jnp.zeros_like(acc_scr)

    u = jnp.dot(h_scr[...], w1_ref[...], preferred_element_type=F32)
    u = jnp.square(jnp.maximum(u, 0.0))
    acc_scr[...] += jnp.dot(u.astype(BF16), w2_ref[...], preferred_element_type=F32)

    @pl.when(f == pl.num_programs(1) - 1)
    def _():
        o_ref[...] = x_ref[...] + gt_ref[0] * acc_scr[...]


def _ffn(x, g, scale, shift, gate, w1, w2, rows_per_batch, *, tm=512, tf=1024):
    M, K = x.shape
    F = w1.shape[1]
    nb = scale.shape[0]
    tm = _row_tile(tm, M, nb, rows_per_batch)
    assert F % tf == 0
    midx = _mod_index(nb, rows_per_batch, tm)
    return pl.pallas_call(
        _ffn_kernel,
        grid=(M // tm, F // tf),
        in_specs=[pl.BlockSpec((tm, K), lambda i, j: (i, 0)),
                  pl.BlockSpec((1, K), lambda i, j: (0, 0)),
                  pl.BlockSpec((1, 1, K), midx),
                  pl.BlockSpec((1, 1, K), midx),
                  pl.BlockSpec((1, 1, K), midx),
                  pl.BlockSpec((K, tf), lambda i, j: (0, j)),
                  pl.BlockSpec((tf, K), lambda i, j: (j, 0))],
        out_specs=pl.BlockSpec((tm, K), lambda i, j: (i, 0)),
        out_shape=jax.ShapeDtypeStruct((M, K), F32),
        scratch_shapes=[pltpu.VMEM((tm, K), BF16), pltpu.VMEM((tm, K), F32)],
        compiler_params=_params(("parallel", "arbitrary")),
        name="ffn",
    )(x, g.reshape(1, K), scale, shift, gate, w1, w2)


def _bdot(a, b, dims=((1,), (0,))):
    return lax.dot_general(a.astype(BF16), b.astype(BF16), (dims, ((), ())),
                           preferred_element_type=F32)


_NT = ((1,), (1,))
_TN = ((0,), (0,))
PAIR = 2 * HEAD_RWKV


def _scan_kernel(z0_ref, zp0_ref, zn0_ref, lo0_ref, z1_ref, zp1_ref, zn1_ref, lo1_ref,
                 cw_ref, kkg_ref, ka_ref, rk_ref, w0_ref, a0_ref, wup_ref, aup_ref, *rest,
                 chunk, pairs, n_chunks, zero_state):
    s0_ref = None if zero_state else rest[0]
    y0_ref, y1_ref, bon0_ref, bon1_ref, sf_ref, s_scr = rest[0 if zero_state else 1:]
    c = pl.program_id(1)
    C = chunk
    N = HEAD_RWKV

    @pl.when(c == 0)
    def _():
        if s0_ref is None:
            s_scr[...] = jnp.zeros(s_scr.shape, F32)
        else:
            for d in range(N_DIR):
                for p in range(pairs):
                    s_scr[d, p] = jnp.concatenate([s0_ref[0, d, 2 * p], s0_ref[0, d, 2 * p + 1]],
                                                  axis=1)

    row = lax.broadcasted_iota(jnp.int32, (C, PAIR), 0)
    lane = lax.broadcasted_iota(jnp.int32, (C, PAIR), 1)
    col = lane & (N - 1)
    eye = row == col
    head0 = lane < N
    ti = lax.broadcasted_iota(jnp.int32, (C, C), 0)
    tj = lax.broadcasted_iota(jnp.int32, (C, C), 1)

    def bd(x):
        return jnp.concatenate([jnp.where(head0, x, 0.0), jnp.where(head0, 0.0, x)],
                               axis=0).astype(BF16)

    def diag_blocks(x):
        return jnp.where(head0, x[:N], x[N:])

    def head_sum(x):
        s0 = jnp.sum(jnp.where(head0, x, 0.0), axis=-1, keepdims=True)
        s1 = jnp.sum(jnp.where(head0, 0.0, x), axis=-1, keepdims=True)
        return jnp.where(head0, s0, s1)

    def direction(d, z_ref, zp_ref, zn_ref, lo_ref, y_ref, bon_ref):
        tt = c if d == 0 else n_chunks - 1 - c
        first = tt == 0
        last = tt == n_chunks - 1
        if d == 0:
            incl, strict, tri = row >= col, row > col, ti >= tj
        else:
            incl, strict, tri = row <= col, row < col, ti <= tj
        tri = jnp.where(tri, 1.0, 0.0).astype(BF16)
        lora = lo_ref[0].astype(BF16)

        def conv(off, sl):
            zs = slice(off + sl.start, off + sl.stop)
            zc = z_ref[0, :, zs]
            before = jnp.where(first, 0.0, zp_ref[0, 7:8, zs])
            after = jnp.where(last, 0.0, zn_ref[0, 0:1, zs])
            zm = jnp.where(row == 0, before, pltpu.roll(zc, 1, 0))
            zq = jnp.where(row == C - 1, after, pltpu.roll(zc, C - 1, 0))
            return zm * cw_ref[0:1, zs] + zc * cw_ref[1:2, zs] + zq * cw_ref[2:3, zs]

        def s_lora(st, sl, p):
            up = jnp.dot(lora, jnp.concatenate([wup_ref[d, :, sl], aup_ref[d, :, sl]], axis=1),
                         preferred_element_type=F32)
            u = w0_ref[d, :, sl] + up[:, :PAIR]
            softplus = jnp.maximum(-u, 0.0) + jnp.log(1.0 + jnp.exp(-jnp.abs(u)))
            st['ld'] = -jnp.exp(-softplus - 0.5)
            st['a'] = _sigmoid(a0_ref[d, :, sl] + up[:, PAIR:])

        def s_cum(st, sl, p):
            ld = st['ld']
            ld_hi = ld.astype(BF16)
            rem = ld - ld_hi.astype(F32)
            ld_mid = rem.astype(BF16)
            ld_lo = (rem - ld_mid.astype(F32)).astype(BF16)
            pieces = jnp.dot(tri, jnp.concatenate([ld_hi, ld_mid, ld_lo], axis=1),
                             preferred_element_type=F32)
            st['cum'] = pieces[:, :PAIR] + pieces[:, PAIR:2 * PAIR] + pieces[:, 2 * PAIR:]
            st['tot'] = jnp.sum(ld, axis=0, keepdims=True)

        def s_conv(st, sl, p):
            st['R'] = conv(0, sl)
            st['Kraw'] = conv(R_DIM, sl)
            st['V'] = conv(2 * R_DIM, sl)

        def s_keys(st, sl, p):
            kx = st['Kraw'] * kkg_ref[:, sl]
            kk = kx * lax.rsqrt(head_sum(kx * kx) + 1e-12)
            a = st.pop('a')
            st['K'] = st.pop('Kraw') * (1.0 + (a - 1.0) * ka_ref[:, sl])
            st['Bv'] = kk * a
            st['kk'] = kk
            bon_ref[0, :, sl] = head_sum(st['R'] * st['K'] * rk_ref[:, sl]) * st['V']

        def s_decay(st, sl, p):
            cum, ld, tot = st.pop('cum'), st.pop('ld'), st['tot']
            e_neg = jnp.exp(-cum)
            e_end = jnp.exp(tot - cum)
            Bv, K = st.pop('Bv'), st.pop('K')
            st['At'] = -st.pop('kk') * jnp.exp(cum - ld)
            st['Rt'] = st.pop('R') * jnp.exp(cum)
            st['b_d'] = Bv * e_end
            st['k_d'] = K * e_end
            st['Bt'] = Bv * e_neg
            st['Kt'] = K * e_neg

        def s_scores(st, sl, p):
            st['bdv'] = bd(st['V'])
            st['sc'] = _bdot(jnp.concatenate([st['At'], st['Rt']], axis=0),
                             jnp.concatenate([bd(st.pop('Bt')), bd(st.pop('Kt'))], axis=0),
                             _NT)

        def s_masks(st, sl, p):
            sc = st.pop('sc')
            L = jnp.where(strict, sc[:C, :PAIR], 0.0)
            st['p_rb'] = jnp.where(incl, sc[C:, :PAIR], 0.0)
            st['p_rk'] = jnp.where(incl, sc[C:, PAIR:], 0.0)
            st['akv'] = _bdot(jnp.where(strict, sc[:C, PAIR:], 0.0), st['bdv'])
            st['x'] = jnp.where(eye, 1.0, 0.0) + L
            st['lp'] = _bdot(L, bd(L))

        def s_double(st, sl, p):
            o = _bdot(st['lp'], jnp.concatenate([bd(st['x']), bd(st['lp'])], axis=1))
            st['x'] = st['x'] + o[:, :PAIR]
            st['lp'] = o[:, PAIR:]

        def s_double_last(st, sl, p):
            st['x'] = st['x'] + _bdot(st.pop('lp'), bd(st['x']))

        def s_solve(st, sl, p):
            z = _bdot(st.pop('x'), jnp.concatenate([bd(st.pop('At')), bd(st.pop('akv'))], axis=1))
            st['a_p'], st['u_loc'] = z[:, :PAIR], z[:, PAIR:]

        def s_local(st, sl, p):
            a_p, u_loc = st.pop('a_p'), st.pop('u_loc')
            f = _bdot(st.pop('p_rb'), jnp.concatenate([bd(a_p), bd(u_loc)], axis=1))
            st['r_p'] = st.pop('Rt') + f[:, :PAIR]
            st['y_loc'] = f[:, PAIR:] + _bdot(st.pop('p_rk'), st.pop('bdv'))
            st['g'] = diag_blocks(_bdot(a_p, st['b_d'], _TN))
            st['s_loc'] = diag_blocks(_bdot(jnp.concatenate([u_loc, st.pop('V')], axis=0),
                                            jnp.concatenate([st.pop('b_d'), st.pop('k_d')], axis=0),
                                            _TN))

        def s_state(st, sl, p):
            S = s_scr[d, p]
            y_ref[0, :, sl] = _bdot(st.pop('r_p'), bd(S), _NT) + st.pop('y_loc')
            s_scr[d, p] = S * jnp.exp(st.pop('tot')) + _bdot(S, bd(st.pop('g'))) + st.pop('s_loc')

        prep = [s_lora, s_cum, s_conv, s_keys, s_decay]
        matrix = [s_scores, s_masks]
        n = 2
        while n < C:
            matrix.append(s_double if 2 * n < C else s_double_last)
            n *= 2
        matrix += [s_solve, s_local, s_state]
        return prep, matrix

    sls = [slice(p * PAIR, (p + 1) * PAIR) for p in range(pairs)]
    prep0, mat0 = direction(0, z0_ref, zp0_ref, zn0_ref, lo0_ref, y0_ref, bon0_ref)
    prep1, mat1 = direction(1, z1_ref, zp1_ref, zn1_ref, lo1_ref, y1_ref, bon1_ref)
    st0 = [dict() for _ in range(pairs)]
    st1 = [dict() for _ in range(pairs)]

    for fn in prep0:
        for p in range(pairs):
            fn(st0[p], sls[p], p)
    prep1_calls = [(fn, p) for fn in prep1 for p in range(pairs)]
    per = -(-len(prep1_calls) // len(mat0))
    for k, fn in enumerate(mat0):
        for p in range(pairs):
            fn(st0[p], sls[p], p)
        for pf, p in prep1_calls[k * per:(k + 1) * per]:
            pf(st1[p], sls[p], p)
    for fn in mat1:
        for p in range(pairs):
            fn(st1[p], sls[p], p)

    @pl.when(c == n_chunks - 1)
    def _():
        for d in range(N_DIR):
            for p in range(pairs):
                sf_ref[0, d, 2 * p] = s_scr[d, p, :, :N]
                sf_ref[0, d, 2 * p + 1] = s_scr[d, p, :, N:]


def _rwkv_scan(z_rkv, lora, p, s0):
    B, T, _ = z_rkv.shape
    C = SCAN_CHUNK
    nC = T // C
    n_pairs = H_RWKV // 2
    N = HEAD_RWKV
    hb = C // 8

    fwd = lambda cc: cc
    bwd = lambda cc: nC - 1 - cc

    def chunk_specs(tm, dd):
        return [
            pl.BlockSpec((1, C, 3 * R_DIM), lambda bb, cc: (bb, tm(cc), 0)),
            pl.BlockSpec((1, 8, 3 * R_DIM), lambda bb, cc: (bb, jnp.maximum(tm(cc) * hb - 1, 0), 0)),
            pl.BlockSpec((1, 8, 3 * R_DIM),
                         lambda bb, cc: (bb, jnp.minimum((tm(cc) + 1) * hb, T // 8 - 1), 0)),
            pl.BlockSpec((1, C, 128), lambda bb, cc: (bb, tm(cc), dd))]

    row3 = lambda a: a.reshape(1, -1)
    whole = lambda shape: pl.BlockSpec(shape, lambda bb, cc: (0,) * len(shape))
    vec = whole((1, R_DIM))
    out0 = pl.BlockSpec((1, C, R_DIM), lambda bb, cc: (bb, fwd(cc), 0))
    out1 = pl.BlockSpec((1, C, R_DIM), lambda bb, cc: (bb, bwd(cc), 0))
    state = pl.BlockSpec((1, N_DIR, H_RWKV, N, N), lambda bb, cc: (bb, 0, 0, 0, 0))
    in_specs = (chunk_specs(fwd, 0) + chunk_specs(bwd, 1)
                + [whole((3, 3 * R_DIM)), vec, vec, vec, whole((N_DIR, 1, R_DIM)), whole((N_DIR, 1, R_DIM)),
                   whole((N_DIR, 128, R_DIM)), whole((N_DIR, 128, R_DIM))]
                + ([] if s0 is None else [state]))
    ydir = jax.ShapeDtypeStruct((B, T, R_DIM), F32)
    y0, y1, b0, b1, s_fin = pl.pallas_call(
        functools.partial(_scan_kernel, chunk=C, pairs=n_pairs, n_chunks=nC, zero_state=s0 is None),
        grid=(B, nC),
        in_specs=in_specs,
        out_specs=[out0, out1, out0, out1, state],
        out_shape=[ydir, ydir, ydir, ydir, jax.ShapeDtypeStruct((B, N_DIR, H_RWKV, N, N), F32)],
        scratch_shapes=[pltpu.VMEM((N_DIR, n_pairs, N, PAIR), F32)],
        compiler_params=_params(("parallel", "arbitrary")),
        name="rwkv_scan",
    )(z_rkv, z_rkv, z_rkv, lora, z_rkv, z_rkv, z_rkv, lora, p['conv_rkv'], row3(p['k_k']),
      row3(p['k_a']), row3(p['r_k']), p['w0'].reshape(N_DIR, 1, R_DIM), p['a0'].reshape(N_DIR, 1, R_DIM),
      p['w_up_pad'], p['a_up_pad'], *([] if s0 is None else [s0]))
    return (y0, y1, b0, b1), s_fin


def _post_kernel(y0_ref, y1_ref, b0_ref, b1_ref, gd_ref, gup_ref, lw_ref, lb_ref, o_ref):
    tm = o_ref.shape[0]
    g = jnp.dot(gd_ref[...].astype(BF16), gup_ref[...], preferred_element_type=F32)
    head0 = lax.broadcasted_iota(jnp.int32, (tm, PAIR), 1) < HEAD_RWKV

    def head_mean(x):
        s0 = jnp.sum(jnp.where(head0, x, 0.0), axis=-1, keepdims=True)
        s1 = jnp.sum(jnp.where(head0, 0.0, x), axis=-1, keepdims=True)
        return jnp.where(head0, s0, s1) * (1.0 / HEAD_RWKV)

    for p in range(R_DIM // PAIR):
        sl = slice(p * PAIR, (p + 1) * PAIR)
        ys = y0_ref[:, sl] + y1_ref[:, sl]
        dev = ys - head_mean(ys)
        yn = dev * lax.rsqrt(head_mean(dev * dev) + LNX_EPS)
        o = yn * lw_ref[:, sl] + lb_ref[:, sl] + (b0_ref[:, sl] + b1_ref[:, sl])
        o_ref[:, sl] = (o * g[:, sl]).astype(o_ref.dtype)


def _rwkv_post(y0, y1, b0, b1, lora, p, *, tm=512):
    M, Rd = y0.shape
    tm = min(tm, M)
    assert M % tm == 0
    rows = pl.BlockSpec((tm, Rd), lambda i: (i, 0))
    vec = pl.BlockSpec((1, Rd), lambda i: (0, 0))
    return pl.pallas_call(
        _post_kernel,
        grid=(M // tm,),
        in_specs=[rows, rows, rows, rows, pl.BlockSpec((tm, 128), lambda i: (i, 2)),
                  pl.BlockSpec((G_LORA, Rd), lambda i: (0, 0)), vec, vec],
        out_specs=pl.BlockSpec((tm, Rd), lambda i: (i, 0)),
        out_shape=jax.ShapeDtypeStruct((M, Rd), BF16),
        compiler_params=_params(("parallel",)),
        name="rwkv_post",
    )(y0, y1, b0, b1, lora, p['g_up_bf16'], p['lnx_w'].reshape(1, Rd), p['lnx_b'].reshape(1, Rd))


def _rope_tables(T, n_cache):
    rows = T // GRID_W
    row = jnp.repeat(jnp.arange(rows, dtype=F32), GRID_W)
    col = jnp.tile(jnp.arange(GRID_W, dtype=F32), rows)
    inv_freq = jnp.power(ROPE_THETA, -jnp.arange(AXIS_PAIRS, dtype=F32) / AXIS_PAIRS)
    ar, ac = row[:, None] * inv_freq, col[:, None] * inv_freq
    cos = jnp.concatenate([jnp.cos(ar), jnp.cos(ar), jnp.cos(ac), jnp.cos(ac)], axis=1)
    sin = jnp.concatenate([-jnp.sin(ar), jnp.sin(ar), -jnp.sin(ac), jnp.sin(ac)], axis=1)
    cos = jnp.concatenate([cos, jnp.ones((n_cache, ROPE_DIM), F32)], axis=0)
    sin = jnp.concatenate([sin, jnp.zeros((n_cache, ROPE_DIM), F32)], axis=0)
    return jnp.tile(cos, (1, 2)), jnp.tile(sin, (1, 2))


def _trunk_layer(x, mod, p, cache):
    B, T, _ = x.shape
    M = B * T
    nb = mod.shape[0]
    shift1, scale1, gate1, shift2, scale2, gate2 = [
        m.reshape(nb, 1, D_MODEL) for m in jnp.split(mod, 6, axis=-1)]
    xf = x.reshape(M, D_MODEL)
    latent = cache is not None
    proj = functools.partial(_norm_matmul, xf, p['norm1'], scale1, shift1, rows_per_batch=T)

    q_tm = 512
    extra, extra_specs = [p['q_gain']], [lambda tm: pl.BlockSpec((1, 2 * Q_PAIR), lambda i, j: (0, 0))]
    if latent:
        cos, sin = _rope_tables(T, cache[0].shape[1])
        n_t = T // min(q_tm, T)
        tbl = lambda tm: pl.BlockSpec((tm, 128), lambda i, j: (i % n_t, 0))
        extra += [cos[:T], sin[:T]]
        extra_specs += [tbl, tbl]
    q, = proj(p['w_q'], tm=q_tm, tn=2 * Q_PAIR, epilogue=functools.partial(_epi_q, rope=latent),
              outs=[(2 * Q_PAIR, Q_DIM, BF16)], extra=extra, extra_specs=extra_specs, row_split=4,
              name="proj_q")
    z_ckv, z_kr, lora = proj(p['w_small'], tm=1024, tn=SMALL_COLS, epilogue=_epi_small,
                             outs=[(KV_RANK, KV_RANK, F32), (ROPE_DIM, ROPE_DIM, F32), (384, 384, F32)],
                             name="proj_small")
    z_rkv, = proj(p['w_rkv'], tm=1024, tn=1024, epilogue=_epi_plain,
                  outs=[(1024, 3 * R_DIM, F32)], name="proj_rkv")
    gates, = proj(p['w_gate'], tm=1024, tn=1024, epilogue=_epi_sigmoid,
                  outs=[(1024, 2 * D_MODEL, BF16)], name="proj_gate")

    if latent:
        ckv_ctx, kr_ctx, s0 = cache
        S = T + ckv_ctx.shape[1]
        ckv_all = jnp.concatenate([z_ckv.reshape(B, T, KV_RANK), ckv_ctx], axis=1).reshape(B * S, KV_RANK)
        kr_all = jnp.concatenate([z_kr.reshape(B, T, ROPE_DIM), kr_ctx], axis=1).reshape(B * S, ROPE_DIM)
        k, v = _keys_values(ckv_all, kr_all, p, (cos, sin), tm=S)
    else:
        S = T
        s0 = None
        k, v = _keys_values(z_ckv, z_kr, p, None, tm=1024)
    o_mla = _attention(q.reshape(B, T, Q_DIM), k, v, S, pairs=H_MLA // 2).reshape(M, MLA_WIDTH)

    yb, s_final = _rwkv_scan(z_rkv.reshape(B, T, 3 * R_DIM), lora.reshape(B, T, 384), p, s0)
    o_rwkv = _rwkv_post(*[a.reshape(M, R_DIM) for a in yb], lora, p)

    merged = _branch_merge(o_mla, o_rwkv, gates, p['w_br_mla'], p['w_br_rwkv'])
    x1 = _out_proj(merged, p['w_out'], xf, gate1, T)
    x2 = _ffn(x1, p['norm2'], scale2, shift2, gate2, p['w_ff_in'], p['w_ff_out'], T)
    return (x2.reshape(B, T, D_MODEL),
            (z_ckv.reshape(B, T, KV_RANK), z_kr.reshape(B, T, ROPE_DIM), s_final))


def _split_w_in(w):
    offs = [sum(IN_SIZES[:i]) for i in range(len(IN_SIZES))]
    part = lambda i: w[:, offs[i]:offs[i] + IN_SIZES[i]]
    K = w.shape[0]
    wq = part(0).reshape(K, H_MLA // 2, 2, QK_HEAD)
    wq = jnp.concatenate([wq[..., :NOPE_DIM].reshape(K, H_MLA // 2, 2 * NOPE_DIM),
                          wq[..., NOPE_DIM:].reshape(K, H_MLA // 2, 2 * ROPE_DIM)], axis=-1)
    wd, ad = part(4), part(5)
    small = jnp.concatenate([part(1), part(2), jnp.zeros((K, 128 - ROPE_DIM), w.dtype),
                             wd[:, :W_LORA], ad[:, :A_LORA], wd[:, W_LORA:], ad[:, A_LORA:],
                             part(6)], axis=1)
    return (wq.reshape(K, Q_DIM).astype(BF16), small.astype(BF16), part(3).astype(BF16),
            part(7).astype(BF16))


def _layer_params(l, w_in, q_norm, w_kv_up, w_br_mla, w_br_rwkv, w_out, w_ff_in, w_ff_out, **small):
    zeros = jnp.zeros((N_DIR, W_LORA, R_DIM), F32)
    w_q, w_small, w_rkv, w_gate = _split_w_in(w_in[l])
    qg = q_norm[l] * (QK_HEAD ** -0.5)
    q_gain = jnp.concatenate([qg[:NOPE_DIM], qg[:NOPE_DIM], qg[NOPE_DIM:], qg[NOPE_DIM:]])
    p = {name: val[l] for name, val in small.items()}
    p.update({
        'w_q': w_q, 'w_small': w_small, 'w_rkv': w_rkv, 'w_gate': w_gate,
        'q_gain': jnp.tile(q_gain, 2).reshape(1, 2 * Q_PAIR),
        'w_kv_up': w_kv_up[l].astype(BF16),
        'w_up_pad': jnp.concatenate([small['w_up'][l], zeros], axis=1).astype(BF16),
        'a_up_pad': jnp.concatenate([zeros, small['a_up'][l]], axis=1).astype(BF16),
        'g_up_bf16': small['g_up'][l].astype(BF16),
        'w_br_mla': w_br_mla[l].astype(BF16), 'w_br_rwkv': w_br_rwkv[l].astype(BF16),
        'w_out': w_out[l].astype(BF16),
        'w_ff_in': w_ff_in[l].astype(BF16), 'w_ff_out': w_ff_out[l].astype(BF16),
    })
    return p


def kernel(x_prompt, x_sample, cache_mla_ckv, cache_mla_kr, state_rwkv, c, c_ctx,
           norm1, w_ada, b_ada, w_in, q_norm, kv_norm, w_kv_up, k_norm, conv_rkv,
           k_k, k_a, r_k, w0, w_up, a0, a_up, g_up, lnx_w, lnx_b,
           w_br_mla, w_br_rwkv, w_out, norm2, w_ff_in, w_ff_out):
    x_p, x_s = x_prompt, x_sample
    n_lat = c.shape[0]
    ckv_list, kr_list, st_list = [], [], []
    for l in range(DEPTH):
        p = _layer_params(l, w_in, q_norm, w_kv_up, w_br_mla, w_br_rwkv, w_out, w_ff_in, w_ff_out,
                          norm1=norm1, kv_norm=kv_norm, k_norm=k_norm, conv_rkv=conv_rkv, k_k=k_k,
                          k_a=k_a, r_k=r_k, w0=w0, w_up=w_up, a0=a0, a_up=a_up, g_up=g_up,
                          lnx_w=lnx_w, lnx_b=lnx_b, norm2=norm2)
        cond = jnp.concatenate([c, c_ctx[None]], axis=0)
        cond = jnp.pad(jax.nn.silu(cond), ((0, (-cond.shape[0]) % 16), (0, 0)))
        mod = _matmul(cond, w_ada[l], tm=16, tn=1024)[:n_lat + 1] + b_ada[l]
        x_p, (ckv_l, kr_l, st_l) = _trunk_layer(x_p, mod[n_lat:], p, None)
        ckv_list.append(ckv_l)
        kr_list.append(kr_l)
        st_list.append(st_l)
        x_s, _ = _trunk_layer(x_s, mod[:n_lat], p,
                              (cache_mla_ckv[:, l], cache_mla_kr[:, l], state_rwkv[:, l]))
    return (x_p, x_s, jnp.stack(ckv_list, axis=1), jnp.stack(kr_list, axis=1),
            jnp.stack(st_list, axis=1))
```

```python
import functools

import jax
import jax.numpy as jnp
from jax import lax
from jax.experimental import pallas as pl
from jax.experimental.pallas import tpu as pltpu

D_MODEL = 2048
DEPTH = 1
GRID_W = 64
H_MLA = 16
NOPE_DIM = 128
ROPE_DIM = 64
QK_HEAD = NOPE_DIM + ROPE_DIM
V_HEAD = 128
KV_RANK = 512
ROPE_THETA = 10000.0
AXIS_DIM = ROPE_DIM // 2
AXIS_PAIRS = AXIS_DIM // 2
MLA_WIDTH = H_MLA * V_HEAD
Q_DIM = H_MLA * QK_HEAD
HEAD_RWKV = 64
H_RWKV = D_MODEL // HEAD_RWKV
R_DIM = H_RWKV * HEAD_RWKV
W_LORA = 64
A_LORA = 64
G_LORA = 128
N_DIR = 2
LNX_EPS = 64e-5
D_FF = 4 * D_MODEL
EPS = 1e-6
IN_SIZES = (Q_DIM, KV_RANK, ROPE_DIM, 3 * R_DIM, N_DIR * W_LORA, N_DIR * A_LORA, G_LORA, 2 * D_MODEL)
IN_DIM = sum(IN_SIZES)

F32 = jnp.float32
BF16 = jnp.bfloat16
VMEM_LIMIT = 56 * 1024 * 1024
SCAN_CHUNK = 64
SCAN_PAIRS = 16


def _params(sem):
    return pltpu.CompilerParams(dimension_semantics=sem, vmem_limit_bytes=VMEM_LIMIT)


def _mm_kernel(a_ref, b_ref, o_ref):
    o_ref[...] = jnp.dot(a_ref[...].astype(BF16), b_ref[...].astype(BF16),
                         preferred_element_type=F32).astype(o_ref.dtype)


def _matmul(a, b, *, tm=512, tn=512, out_dtype=F32):
    M, K = a.shape
    N = b.shape[1]
    tm = min(tm, M)
    tn = min(tn, N)
    assert M % tm == 0 and N % tn == 0, (M, N, tm, tn)
    return pl.pallas_call(
        _mm_kernel,
        grid=(M // tm, N // tn),
        in_specs=[pl.BlockSpec((tm, K), lambda i, j: (i, 0)),
                  pl.BlockSpec((K, tn), lambda i, j: (0, j))],
        out_specs=pl.BlockSpec((tm, tn), lambda i, j: (i, j)),
        out_shape=jax.ShapeDtypeStruct((M, N), out_dtype),
        compiler_params=_params(("parallel", "arbitrary")),
        name="matmul",
    )(a, b)


def _modulated_norm(x, g, scale, shift):
    ms = jnp.mean(x * x, axis=-1, keepdims=True)
    return (x * lax.rsqrt(ms + EPS) * g) * (1.0 + scale) + shift


def _mod_index(nb, rows_per_batch, tm):
    if nb == 1:
        return lambda i, j: (0, 0, 0)
    assert rows_per_batch % tm == 0
    return lambda i, j: (i * tm // rows_per_batch, 0, 0)


def _row_tile(tm, M, nb, rows_per_batch):
    tm = min(tm, M, rows_per_batch) if nb > 1 else min(tm, M)
    assert M % tm == 0
    return tm


def _sigmoid(x):
    return 1.0 / (1.0 + jnp.exp(-x))


def _swap16(x):
    lane = lax.broadcasted_iota(jnp.int32, x.shape, 1)
    return jnp.where((lane & AXIS_PAIRS) == 0,
                     pltpu.roll(x, 128 - AXIS_PAIRS, 1), pltpu.roll(x, AXIS_PAIRS, 1))


def _epi_plain(acc, extra, outs, rows):
    outs[0][rows, :] = acc.astype(outs[0].dtype)


def _epi_sigmoid(acc, extra, outs, rows):
    outs[0][rows, :] = _sigmoid(acc).astype(outs[0].dtype)


SMALL_LORA = KV_RANK + 128
SMALL_COLS = SMALL_LORA + 3 * 128


def _epi_small(acc, extra, outs, rows):
    ckv_ref, kr_ref, lora_ref = outs
    ckv_ref[rows, :] = acc[:, :KV_RANK]
    kr_ref[rows, :] = acc[:, KV_RANK:KV_RANK + ROPE_DIM]
    o = SMALL_LORA
    wa = acc[:, o:o + 256]
    lane = lax.broadcasted_iota(jnp.int32, wa.shape, 1)
    wa = jnp.where((lane & W_LORA) == 0, jnp.tanh(wa), wa)
    lora_ref[rows, :] = jnp.concatenate([wa, _sigmoid(acc[:, o + 256:o + 384])], axis=1)


Q_PAIR = 2 * QK_HEAD


def _epi_q(acc, extra, outs, rows, *, rope):
    g_ref = extra[0]
    tm, tn = acc.shape
    lo = lax.broadcasted_iota(jnp.int32, (tm, 128), 1) < ROPE_DIM
    parts = []
    for p in range(tn // Q_PAIR):
        z = acc[:, p * Q_PAIR:(p + 1) * Q_PAIR]
        g = g_ref[:, p * Q_PAIR:(p + 1) * Q_PAIR]
        n0, n1, rp = z[:, :128], z[:, 128:256], z[:, 256:]
        rp2 = rp * rp
        s0 = (jnp.sum(n0 * n0, axis=-1, keepdims=True)
              + jnp.sum(jnp.where(lo, rp2, 0.0), axis=-1, keepdims=True))
        s1 = (jnp.sum(n1 * n1, axis=-1, keepdims=True)
              + jnp.sum(jnp.where(lo, 0.0, rp2), axis=-1, keepdims=True))
        r0 = lax.rsqrt(s0 * (1.0 / QK_HEAD) + EPS)
        r1 = lax.rsqrt(s1 * (1.0 / QK_HEAD) + EPS)
        qr = rp * jnp.where(lo, r0, r1) * g[:, 256:]
        if rope:
            qr = qr * extra[1][rows, :] + _swap16(qr) * extra[2][rows, :]
        parts += [n0 * r0 * g[:, :128], n1 * r1 * g[:, 128:256], qr]
    outs[0][rows, :] = jnp.concatenate(parts, axis=1).astype(outs[0].dtype)


def _norm_mm_kernel(*refs, n_extra, n_out, epilogue, row_split):
    x_ref, g_ref, sc_ref, sh_ref, w_ref = refs[:5]
    extra = refs[5:5 + n_extra]
    outs = refs[5 + n_extra:5 + n_extra + n_out]
    h_scr = refs[-1]

    @pl.when(pl.program_id(1) == 0)
    def _():
        h = _modulated_norm(x_ref[...], g_ref[...], sc_ref[0], sh_ref[0])
        h_scr[...] = h.astype(BF16)

    step = h_scr.shape[0] // row_split
    for r in range(row_split):
        rows = pl.ds(r * step, step)
        epilogue(jnp.dot(h_scr[rows, :], w_ref[...], preferred_element_type=F32), extra, outs, rows)


def _norm_matmul(x, g, scale, shift, w, rows_per_batch, *, tm, tn, epilogue, outs,
                 extra=(), extra_specs=(), row_split=2, name):
    M, K = x.shape
    N = w.shape[1]
    nb = scale.shape[0]
    tm = _row_tile(tm, M, nb, rows_per_batch)
    assert N % tn == 0 and tm % (16 * row_split) == 0
    midx = _mod_index(nb, rows_per_batch, tm)
    res = pl.pallas_call(
        functools.partial(_norm_mm_kernel, n_extra=len(extra), n_out=len(outs), epilogue=epilogue,
                          row_split=row_split),
        grid=(M // tm, N // tn),
        in_specs=[pl.BlockSpec((tm, K), lambda i, j: (i, 0)),
                  pl.BlockSpec((1, K), lambda i, j: (0, 0)),
                  pl.BlockSpec((1, 1, K), midx),
                  pl.BlockSpec((1, 1, K), midx),
                  pl.BlockSpec((K, tn), lambda i, j: (0, j))] + [s(tm) for s in extra_specs],
        out_specs=[pl.BlockSpec((tm, c), lambda i, j: (i, j)) for c, _, _ in outs],
        out_shape=[jax.ShapeDtypeStruct((M, n), dt) for _, n, dt in outs],
        scratch_shapes=[pltpu.VMEM((tm, K), BF16)],
        compiler_params=_params(("parallel", "arbitrary")),
        name=name,
    )(x, g.reshape(1, K), scale, shift, w, *extra)
    return res


K_WIDTH = 2 * NOPE_DIM
KV_ROW_SPLIT = 4


def _kv_kernel(*refs, rope):
    if rope:
        (ckv_ref, kr_ref, gkv_ref, gk_ref, cos_ref, sin_ref, w_ref, k_ref, v_ref,
         a_scr, kr_scr, ss_scr) = refs
    else:
        ckv_ref, kr_ref, gkv_ref, gk_ref, w_ref, k_ref, v_ref, a_scr, kr_scr, ss_scr = refs
    h = pl.program_id(1)

    @pl.when(h == 0)
    def _():
        c = ckv_ref[...]
        a = c * lax.rsqrt(jnp.mean(c * c, axis=-1, keepdims=True) + EPS) * gkv_ref[...]
        a_scr[...] = a.astype(BF16)
        kr = kr_ref[...]
        ss_scr[...] = jnp.broadcast_to(0.5 * jnp.sum(kr * kr, axis=-1, keepdims=True), ss_scr.shape)
        krg = kr * gk_ref[:, NOPE_DIM:]
        if rope:
            krg = krg * cos_ref[...] + _swap16(krg) * sin_ref[...]
        kr_scr[...] = krg

    step = a_scr.shape[0] // KV_ROW_SPLIT
    for r in range(KV_ROW_SPLIT):
        rows = pl.ds(r * step, step)
        acc = jnp.dot(a_scr[rows, :], w_ref[...], preferred_element_type=F32)
        kn = acc[:, :NOPE_DIM]
        ssq = jnp.sum(kn * kn, axis=-1, keepdims=True) + ss_scr[rows, :1]
        rstd = lax.rsqrt(ssq * (1.0 / QK_HEAD) + EPS)
        lane = lax.broadcasted_iota(jnp.int32, kn.shape, 1)
        mine = (lane >> 6) == (h & 1)
        krh = jnp.where(mine, kr_scr[rows, :] * rstd, 0.0)
        k_ref[0, rows, :] = jnp.concatenate([kn * rstd * gk_ref[:, :NOPE_DIM], krh], axis=1).astype(BF16)
        v_ref[0, rows, :] = acc[:, NOPE_DIM:].astype(BF16)


def _keys_values(ckv, kr, p, tables, *, tm):
    Mk = ckv.shape[0]
    tm = min(tm, Mk)
    assert Mk % tm == 0 and tm % (16 * KV_ROW_SPLIT) == 0
    rope = tables is not None
    kr2 = jnp.concatenate([kr, kr], axis=-1)
    gk = jnp.concatenate([p['k_norm'], p['k_norm'][NOPE_DIM:]]).reshape(1, K_WIDTH)
    row = lambda i, h: (i, 0)
    fixed = lambda i, h: (0, 0)
    in_specs = [pl.BlockSpec((tm, KV_RANK), row), pl.BlockSpec((tm, 128), row),
                pl.BlockSpec((1, KV_RANK), fixed), pl.BlockSpec((1, K_WIDTH), fixed)]
    args = [ckv, kr2, p['kv_norm'].reshape(1, KV_RANK), gk]
    if rope:
        in_specs += [pl.BlockSpec((tm, 128), fixed), pl.BlockSpec((tm, 128), fixed)]
        args += list(tables)
    in_specs.append(pl.BlockSpec((KV_RANK, NOPE_DIM + V_HEAD), lambda i, h: (0, h)))
    args.append(p['w_kv_up'])
    return pl.pallas_call(
        functools.partial(_kv_kernel, rope=rope),
        grid=(Mk // tm, H_MLA),
        in_specs=in_specs,
        out_specs=[pl.BlockSpec((1, tm, K_WIDTH), lambda i, h: (h, i, 0)),
                   pl.BlockSpec((1, tm, V_HEAD), lambda i, h: (h, i, 0))],
        out_shape=[jax.ShapeDtypeStruct((H_MLA, Mk, K_WIDTH), BF16),
                   jax.ShapeDtypeStruct((H_MLA, Mk, V_HEAD), BF16)],
        scratch_shapes=[pltpu.VMEM((tm, KV_RANK), BF16), pltpu.VMEM((tm, 128), F32),
                        pltpu.VMEM((tm, 128), F32)],
        compiler_params=_params(("parallel", "arbitrary")),
        name="keys_values",
    )(*args)


def _attn_kernel(q_ref, k_ref, v_ref, o_ref, *, pairs):
    for pr in range(pairs):
        q = q_ref[0, :, pr * Q_PAIR:(pr + 1) * Q_PAIR]
        qr = q[:, 2 * NOPE_DIM:]
        for h in range(2):
            hh = 2 * pr + h
            qh = jnp.concatenate([q[:, h * NOPE_DIM:(h + 1) * NOPE_DIM], qr], axis=1)
            s = lax.dot_general(qh, k_ref[hh], (((1,), (1,)), ((), ())), preferred_element_type=F32)
            m = jnp.max(s, axis=-1, keepdims=True)
            p = jnp.exp(s - m)
            l = jnp.sum(p, axis=-1, keepdims=True)
            o = jnp.dot(p.astype(BF16), v_ref[hh], preferred_element_type=F32) / l
            o_ref[0, :, hh * V_HEAD:(hh + 1) * V_HEAD] = o.astype(o_ref.dtype)


def _attention(q, k, v, S, *, tq=256, pairs):
    B, T, _ = q.shape
    tq = min(tq, T)
    hp = 2 * pairs
    return pl.pallas_call(
        functools.partial(_attn_kernel, pairs=pairs),
        grid=(B, H_MLA // hp, T // tq),
        in_specs=[pl.BlockSpec((1, tq, pairs * Q_PAIR), lambda b, p, i: (b, i, p)),
                  pl.BlockSpec((hp, S, K_WIDTH), lambda b, p, i: (p, b, 0)),
                  pl.BlockSpec((hp, S, V_HEAD), lambda b, p, i: (p, b, 0))],
        out_specs=pl.BlockSpec((1, tq, hp * V_HEAD), lambda b, p, i: (b, i, p)),
        out_shape=jax.ShapeDtypeStruct((B, T, MLA_WIDTH), BF16),
        compiler_params=_params(("parallel", "parallel", "arbitrary")),
        name="attention",
    )(q, k, v)


def _branch_kernel(a_ref, b_ref, ga_ref, gb_ref, wa_ref, wb_ref, o_ref):
    ya = jnp.dot(a_ref[...], wa_ref[...], preferred_element_type=F32)
    yb = jnp.dot(b_ref[...], wb_ref[...], preferred_element_type=F32)
    o_ref[...] = (ga_ref[...].astype(F32) * ya + gb_ref[...].astype(F32) * yb).astype(o_ref.dtype)


def _branch_merge(a, b, gates, wa, wb, *, tm=1024, tn=1024):
    M, K = a.shape
    N = wa.shape[1]
    tm = min(tm, M)
    nj = N // tn
    return pl.pallas_call(
        _branch_kernel,
        grid=(M // tm, nj),
        in_specs=[pl.BlockSpec((tm, K), lambda i, j: (i, 0)),
                  pl.BlockSpec((tm, K), lambda i, j: (i, 0)),
                  pl.BlockSpec((tm, tn), lambda i, j: (i, j)),
                  pl.BlockSpec((tm, tn), lambda i, j: (i, j + nj)),
                  pl.BlockSpec((K, tn), lambda i, j: (0, j)),
                  pl.BlockSpec((K, tn), lambda i, j: (0, j))],
        out_specs=pl.BlockSpec((tm, tn), lambda i, j: (i, j)),
        out_shape=jax.ShapeDtypeStruct((M, N), BF16),
        compiler_params=_params(("parallel", "arbitrary")),
        name="branch_merge",
    )(a, b, gates, gates, wa, wb)


def _resid_kernel(m_ref, w_ref, x_ref, gt_ref, o_ref):
    o_ref[...] = x_ref[...] + gt_ref[0] * jnp.dot(m_ref[...], w_ref[...],
                                                  preferred_element_type=F32)


def _out_proj(m, w, x, gate, rows_per_batch, *, tm=1024, tn=1024):
    M, K = m.shape
    N = w.shape[1]
    nb = gate.shape[0]
    tm = _row_tile(tm, M, nb, rows_per_batch)
    if nb == 1:
        gidx = lambda i, j: (0, 0, j)
    else:
        gidx = lambda i, j: (i * tm // rows_per_batch, 0, j)
    return pl.pallas_call(
        _resid_kernel,
        grid=(M // tm, N // tn),
        in_specs=[pl.BlockSpec((tm, K), lambda i, j: (i, 0)),
                  pl.BlockSpec((K, tn), lambda i, j: (0, j)),
                  pl.BlockSpec((tm, tn), lambda i, j: (i, j)),
                  pl.BlockSpec((1, 1, tn), gidx)],
        out_specs=pl.BlockSpec((tm, tn), lambda i, j: (i, j)),
        out_shape=jax.ShapeDtypeStruct((M, N), F32),
        compiler_params=_params(("parallel", "arbitrary")),
        name="out_proj",
    )(m, w, x, gate)


def _ffn_kernel(x_ref, g_ref, sc_ref, sh_ref, gt_ref, w1_ref, w2_ref, o_ref, h_scr, acc_scr):
    f = pl.program_id(1)

    @pl.when(f == 0)
    def _():
        h = _modulated_norm(x_ref[...], g_ref[...], sc_ref[0], sh_ref[0])
        h_scr[...] = h.astype(BF16)
        acc_scr[...] = jnp.zeros_like(acc_scr)

    u = jnp.dot(h_scr[...], w1_ref[...], preferred_element_type=F32)
    u = jnp.square(jnp.maximum(u, 0.0))
    acc_scr[...] += jnp.dot(u.astype(BF16), w2_ref[...], preferred_element_type=F32)

    @pl.when(f == pl.num_programs(1) - 1)
    def _():
        o_ref[...] = x_ref[...] + gt_ref[0] * acc_scr[...]


def _ffn(x, g, scale, shift, gate, w1, w2, rows_per_batch, *, tm=512, tf=1024):
    M, K = x.shape
    F = w1.shape[1]
    nb = scale.shape[0]
    tm = _row_tile(tm, M, nb, rows_per_batch)
    assert F % tf == 0
    midx = _mod_index(nb, rows_per_batch, tm)
    return pl.pallas_call(
        _ffn_kernel,
        grid=(M // tm, F // tf),
        in_specs=[pl.BlockSpec((tm, K), lambda i, j: (i, 0)),
                  pl.BlockSpec((1, K), lambda i, j: (0, 0)),
                  pl.BlockSpec((1, 1, K), midx),
                  pl.BlockSpec((1, 1, K), midx),
                  pl.BlockSpec((1, 1, K), midx),
                  pl.BlockSpec((K, tf), lambda i, j: (0, j)),
                  pl.BlockSpec((tf, K), lambda i, j: (j, 0))],
        out_specs=pl.BlockSpec((tm, K), lambda i, j: (i, 0)),
        out_shape=jax.ShapeDtypeStruct((M, K), F32),
        scratch_shapes=[pltpu.VMEM((tm, K), BF16), pltpu.VMEM((tm, K), F32)],
        compiler_params=_params(("parallel", "arbitrary")),
        name="ffn",
    )(x, g.reshape(1, K), scale, shift, gate, w1, w2)


def _bdot(a, b, dims=((1,), (0,))):
    return lax.dot_general(a.astype(BF16), b.astype(BF16), (dims, ((), ())),
                           preferred_element_type=F32)


_NT = ((1,), (1,))
_TN = ((0,), (0,))
PAIR = 2 * HEAD_RWKV


def _scan_kernel(z0_ref, zp0_ref, zn0_ref, lo0_ref, z1_ref, zp1_ref, zn1_ref, lo1_ref,
                 cw_ref, kkg_ref, ka_ref, rk_ref, w0_ref, a0_ref, wup_ref, aup_ref, *rest,
                 chunk, pairs, n_chunks, zero_state):
    s0_ref = None if zero_state else rest[0]
    y0_ref, y1_ref, bon0_ref, bon1_ref, sf_ref, s_scr = rest[0 if zero_state else 1:]
    c = pl.program_id(1)
    C = chunk
    N = HEAD_RWKV

    @pl.when(c == 0)
    def _():
        if s0_ref is None:
            s_scr[...] = jnp.zeros(s_scr.shape, F32)
        else:
            for d in range(N_DIR):
                for p in range(pairs):
                    s_scr[d, p] = jnp.concatenate([s0_ref[0, d, 2 * p], s0_ref[0, d, 2 * p + 1]],
                                                  axis=1)

    row = lax.broadcasted_iota(jnp.int32, (C, PAIR), 0)
    lane = lax.broadcasted_iota(jnp.int32, (C, PAIR), 1)
    col = lane & (N - 1)
    eye = row == col
    head0 = lane < N
    ti = lax.broadcasted_iota(jnp.int32, (C, C), 0)
    tj = lax.broadcasted_iota(jnp.int32, (C, C), 1)

    def bd(x):
        return jnp.concatenate([jnp.where(head0, x, 0.0), jnp.where(head0, 0.0, x)],
                               axis=0).astype(BF16)

    def diag_blocks(x):
        return jnp.where(head0, x[:N], x[N:])

    def head_sum(x):
        s0 = jnp.sum(jnp.where(head0, x, 0.0), axis=-1, keepdims=True)
        s1 = jnp.sum(jnp.where(head0, 0.0, x), axis=-1, keepdims=True)
        return jnp.where(head0, s0, s1)

    def direction(d, z_ref, zp_ref, zn_ref, lo_ref, y_ref, bon_ref):
        tt = c if d == 0 else n_chunks - 1 - c
        first = tt == 0
        last = tt == n_chunks - 1
        if d == 0:
            incl, strict, tri = row >= col, row > col, ti >= tj
        else:
            incl, strict, tri = row <= col, row < col, ti <= tj
        tri = jnp.where(tri, 1.0, 0.0).astype(BF16)
        lora = lo_ref[0].astype(BF16)

        def conv(off, sl):
            zs = slice(off + sl.start, off + sl.stop)
            zc = z_ref[0, :, zs]
            before = jnp.where(first, 0.0, zp_ref[0, 7:8, zs])
            after = jnp.where(last, 0.0, zn_ref[0, 0:1, zs])
            zm = jnp.where(row == 0, before, pltpu.roll(zc, 1, 0))
            zq = jnp.where(row == C - 1, after, pltpu.roll(zc, C - 1, 0))
            return zm * cw_ref[0:1, zs] + zc * cw_ref[1:2, zs] + zq * cw_ref[2:3, zs]

        def s_lora(st, sl, p):
            up = jnp.dot(lora, jnp.concatenate([wup_ref[d, :, sl], aup_ref[d, :, sl]], axis=1),
                         preferred_element_type=F32)
            u = w0_ref[d, :, sl] + up[:, :PAIR]
            st['ld'] = -(2.718281828459045 ** -0.5) * _sigmoid(u)
            st['a'] = _sigmoid(a0_ref[d, :, sl] + up[:, PAIR:])

        def s_cum(st, sl, p):
            ld = st['ld']
            ld_hi = ld.astype(BF16)
            rem = ld - ld_hi.astype(F32)
            ld_mid = rem.astype(BF16)
            ld_lo = (rem - ld_mid.astype(F32)).astype(BF16)
            pieces = jnp.dot(tri, jnp.concatenate([ld_hi, ld_mid, ld_lo], axis=1),
                             preferred_element_type=F32)
            st['cum'] = pieces[:, :PAIR] + pieces[:, PAIR:2 * PAIR] + pieces[:, 2 * PAIR:]
            st['tot'] = jnp.sum(ld, axis=0, keepdims=True)

        def s_conv(st, sl, p):
            st['R'] = conv(0, sl)
            st['Kraw'] = conv(R_DIM, sl)
            st['V'] = conv(2 * R_DIM, sl)

        def s_keys(st, sl, p):
            kx = st['Kraw'] * kkg_ref[:, sl]
            kk = kx * lax.rsqrt(head_sum(kx * kx) + 1e-12)
            a = st.pop('a')
            st['K'] = st.pop('Kraw') * (1.0 + (a - 1.0) * ka_ref[:, sl])
            st['Bv'] = kk * a
            st['kk'] = kk
            bon_ref[0, :, sl] = head_sum(st['R'] * st['K'] * rk_ref[:, sl]) * st['V']

        def s_decay(st, sl, p):
            cum, ld, tot = st.pop('cum'), st.pop('ld'), st['tot']
            e_neg = jnp.exp(-cum)
            e_end = jnp.exp(tot - cum)
            Bv, K = st.pop('Bv'), st.pop('K')
            st['At'] = -st.pop('kk') * jnp.exp(cum - ld)
            st['Rt'] = st.pop('R') * jnp.exp(cum)
            st['b_d'] = Bv * e_end
            st['k_d'] = K * e_end
            st['Bt'] = Bv * e_neg
            st['Kt'] = K * e_neg

        def s_scores(st, sl, p):
            st['bdv'] = bd(st['V'])
            st['sc'] = _bdot(jnp.concatenate([st['At'], st['Rt']], axis=0),
                             jnp.concatenate([bd(st.pop('Bt')), bd(st.pop('Kt'))], axis=0),
                             _NT)

        def s_masks(st, sl, p):
            sc = st.pop('sc')
            L = jnp.where(strict, sc[:C, :PAIR], 0.0)
            st['p_rb'] = jnp.where(incl, sc[C:, :PAIR], 0.0)
            st['p_rk'] = jnp.where(incl, sc[C:, PAIR:], 0.0)
            st['akv'] = _bdot(jnp.where(strict, sc[:C, PAIR:], 0.0), st['bdv'])
            st['x'] = jnp.where(eye, 1.0, 0.0) + L
            st['lp'] = _bdot(L, bd(L))

        def s_double(st, sl, p):
            o = _bdot(st['lp'], jnp.concatenate([bd(st['x']), bd(st['lp'])], axis=1))
            st['x'] = st['x'] + o[:, :PAIR]
            st['lp'] = o[:, PAIR:]

        def s_double_last(st, sl, p):
            st['x'] = st['x'] + _bdot(st.pop('lp'), bd(st['x']))

        def s_solve(st, sl, p):
            z = _bdot(st.pop('x'), jnp.concatenate([bd(st.pop('At')), bd(st.pop('akv'))], axis=1))
            st['a_p'], st['u_loc'] = z[:, :PAIR], z[:, PAIR:]

        def s_local(st, sl, p):
            a_p, u_loc = st.pop('a_p'), st.pop('u_loc')
            f = _bdot(st.pop('p_rb'), jnp.concatenate([bd(a_p), bd(u_loc)], axis=1))
            st['r_p'] = st.pop('Rt') + f[:, :PAIR]
            st['y_loc'] = f[:, PAIR:] + _bdot(st.pop('p_rk'), st.pop('bdv'))
            st['g'] = diag_blocks(_bdot(a_p, st['b_d'], _TN))
            st['s_loc'] = diag_blocks(_bdot(jnp.concatenate([u_loc, st.pop('V')], axis=0),
                                            jnp.concatenate([st.pop('b_d'), st.pop('k_d')], axis=0),
                                            _TN))

        def s_state(st, sl, p):
            S = s_scr[d, p]
            y_ref[0, :, sl] = _bdot(st.pop('r_p'), bd(S), _NT) + st.pop('y_loc')
            s_scr[d, p] = S * jnp.exp(st.pop('tot')) + _bdot(S, bd(st.pop('g'))) + st.pop('s_loc')

        prep = [s_lora, s_cum, s_conv, s_keys, s_decay]
        matrix = [s_scores, s_masks]
        n = 2
        while n < C:
            matrix.append(s_double if 2 * n < C else s_double_last)
            n *= 2
        matrix += [s_solve, s_local, s_state]
        return prep, matrix

    sls = [slice(p * PAIR, (p + 1) * PAIR) for p in range(pairs)]
    prep0, mat0 = direction(0, z0_ref, zp0_ref, zn0_ref, lo0_ref, y0_ref, bon0_ref)
    prep1, mat1 = direction(1, z1_ref, zp1_ref, zn1_ref, lo1_ref, y1_ref, bon1_ref)
    st0 = [dict() for _ in range(pairs)]
    st1 = [dict() for _ in range(pairs)]

    for fn in prep0:
        for p in range(pairs):
            fn(st0[p], sls[p], p)
    prep1_calls = [(fn, p) for fn in prep1 for p in range(pairs)]
    per = -(-len(prep1_calls) // len(mat0))
    for k, fn in enumerate(mat0):
        for p in range(pairs):
            fn(st0[p], sls[p], p)
        for pf, p in prep1_calls[k * per:(k + 1) * per]:
            pf(st1[p], sls[p], p)
    for fn in mat1:
        for p in range(pairs):
            fn(st1[p], sls[p], p)

    @pl.when(c == n_chunks - 1)
    def _():
        for d in range(N_DIR):
            for p in range(pairs):
                sf_ref[0, d, 2 * p] = s_scr[d, p, :, :N]
                sf_ref[0, d, 2 * p + 1] = s_scr[d, p, :, N:]


def _rwkv_scan(z_rkv, lora, p, s0):
    B, T, _ = z_rkv.shape
    C = SCAN_CHUNK
    nC = T // C
    n_pairs = H_RWKV // 2
    N = HEAD_RWKV
    hb = C // 8

    fwd = lambda cc: cc
    bwd = lambda cc: nC - 1 - cc

    def chunk_specs(tm, dd):
        return [
            pl.BlockSpec((1, C, 3 * R_DIM), lambda bb, cc: (bb, tm(cc), 0)),
            pl.BlockSpec((1, 8, 3 * R_DIM), lambda bb, cc: (bb, jnp.maximum(tm(cc) * hb - 1, 0), 0)),
            pl.BlockSpec((1, 8, 3 * R_DIM),
                         lambda bb, cc: (bb, jnp.minimum((tm(cc) + 1) * hb, T // 8 - 1), 0)),
            pl.BlockSpec((1, C, 128), lambda bb, cc: (bb, tm(cc), dd))]

    row3 = lambda a: a.reshape(1, -1)
    whole = lambda shape: pl.BlockSpec(shape, lambda bb, cc: (0,) * len(shape))
    vec = whole((1, R_DIM))
    out0 = pl.BlockSpec((1, C, R_DIM), lambda bb, cc: (bb, fwd(cc), 0))
    out1 = pl.BlockSpec((1, C, R_DIM), lambda bb, cc: (bb, bwd(cc), 0))
    state = pl.BlockSpec((1, N_DIR, H_RWKV, N, N), lambda bb, cc: (bb, 0, 0, 0, 0))
    in_specs = (chunk_specs(fwd, 0) + chunk_specs(bwd, 1)
                + [whole((3, 3 * R_DIM)), vec, vec, vec, whole((N_DIR, 1, R_DIM)), whole((N_DIR, 1, R_DIM)),
                   whole((N_DIR, 128, R_DIM)), whole((N_DIR, 128, R_DIM))]
                + ([] if s0 is None else [state]))
    ydir = jax.ShapeDtypeStruct((B, T, R_DIM), F32)
    y0, y1, b0, b1, s_fin = pl.pallas_call(
        functools.partial(_scan_kernel, chunk=C, pairs=n_pairs, n_chunks=nC, zero_state=s0 is None),
        grid=(B, nC),
        in_specs=in_specs,
        out_specs=[out0, out1, out0, out1, state],
        out_shape=[ydir, ydir, ydir, ydir, jax.ShapeDtypeStruct((B, N_DIR, H_RWKV, N, N), F32)],
        scratch_shapes=[pltpu.VMEM((N_DIR, n_pairs, N, PAIR), F32)],
        compiler_params=_params(("parallel", "arbitrary")),
        name="rwkv_scan",
    )(z_rkv, z_rkv, z_rkv, lora, z_rkv, z_rkv, z_rkv, lora, p['conv_rkv'], row3(p['k_k']),
      row3(p['k_a']), row3(p['r_k']), p['w0'].reshape(N_DIR, 1, R_DIM), p['a0'].reshape(N_DIR, 1, R_DIM),
      p['w_up_pad'], p['a_up_pad'], *([] if s0 is None else [s0]))
    return (y0, y1, b0, b1), s_fin


def _post_kernel(y0_ref, y1_ref, b0_ref, b1_ref, gd_ref, gup_ref, lw_ref, lb_ref, o_ref):
    tm = o_ref.shape[0]
    g = jnp.dot(gd_ref[...].astype(BF16), gup_ref[...], preferred_element_type=F32)
    head0 = lax.broadcasted_iota(jnp.int32, (tm, PAIR), 1) < HEAD_RWKV

    def head_mean(x):
        s0 = jnp.sum(jnp.where(head0, x, 0.0), axis=-1, keepdims=True)
        s1 = jnp.sum(jnp.where(head0, 0.0, x), axis=-1, keepdims=True)
        return jnp.where(head0, s0, s1) * (1.0 / HEAD_RWKV)

    for p in range(R_DIM // PAIR):
        sl = slice(p * PAIR, (p + 1) * PAIR)
        ys = y0_ref[:, sl] + y1_ref[:, sl]
        dev = ys - head_mean(ys)
        yn = dev * lax.rsqrt(head_mean(dev * dev) + LNX_EPS)
        o = yn * lw_ref[:, sl] + lb_ref[:, sl] + (b0_ref[:, sl] + b1_ref[:, sl])
        o_ref[:, sl] = (o * g[:, sl]).astype(o_ref.dtype)


def _rwkv_post(y0, y1, b0, b1, lora, p, *, tm=512):
    M, Rd = y0.shape
    tm = min(tm, M)
    assert M % tm == 0
    rows = pl.BlockSpec((tm, Rd), lambda i: (i, 0))
    vec = pl.BlockSpec((1, Rd), lambda i: (0, 0))
    return pl.pallas_call(
        _post_kernel,
        grid=(M // tm,),
        in_specs=[rows, rows, rows, rows, pl.BlockSpec((tm, 128), lambda i: (i, 2)),
                  pl.BlockSpec((G_LORA, Rd), lambda i: (0, 0)), vec, vec],
        out_specs=pl.BlockSpec((tm, Rd), lambda i: (i, 0)),
        out_shape=jax.ShapeDtypeStruct((M, Rd), BF16),
        compiler_params=_params(("parallel",)),
        name="rwkv_post",
    )(y0, y1, b0, b1, lora, p['g_up_bf16'], p['lnx_w'].reshape(1, Rd), p['lnx_b'].reshape(1, Rd))


def _rope_tables(T, n_cache):
    rows = T // GRID_W
    row = jnp.repeat(jnp.arange(rows, dtype=F32), GRID_W)
    col = jnp.tile(jnp.arange(GRID_W, dtype=F32), rows)
    inv_freq = jnp.power(ROPE_THETA, -jnp.arange(AXIS_PAIRS, dtype=F32) / AXIS_PAIRS)
    ar, ac = row[:, None] * inv_freq, col[:, None] * inv_freq
    cos = jnp.concatenate([jnp.cos(ar), jnp.cos(ar), jnp.cos(ac), jnp.cos(ac)], axis=1)
    sin = jnp.concatenate([-jnp.sin(ar), jnp.sin(ar), -jnp.sin(ac), jnp.sin(ac)], axis=1)
    cos = jnp.concatenate([cos, jnp.ones((n_cache, ROPE_DIM), F32)], axis=0)
    sin = jnp.concatenate([sin, jnp.zeros((n_cache, ROPE_DIM), F32)], axis=0)
    return jnp.tile(cos, (1, 2)), jnp.tile(sin, (1, 2))


def _trunk_layer(x, mod, p, cache):
    B, T, _ = x.shape
    M = B * T
    nb = mod.shape[0]
    shift1, scale1, gate1, shift2, scale2, gate2 = [
        m.reshape(nb, 1, D_MODEL) for m in jnp.split(mod, 6, axis=-1)]
    xf = x.reshape(M, D_MODEL)
    latent = cache is not None
    proj = functools.partial(_norm_matmul, xf, p['norm1'], scale1, shift1, rows_per_batch=T)

    q_tm = 512
    extra, extra_specs = [p['q_gain']], [lambda tm: pl.BlockSpec((1, 2 * Q_PAIR), lambda i, j: (0, 0))]
    if latent:
        cos, sin = _rope_tables(T, cache[0].shape[1])
        n_t = T // min(q_tm, T)
        tbl = lambda tm: pl.BlockSpec((tm, 128), lambda i, j: (i % n_t, 0))
        extra += [cos[:T], sin[:T]]
        extra_specs += [tbl, tbl]
    q, = proj(p['w_q'], tm=q_tm, tn=2 * Q_PAIR, epilogue=functools.partial(_epi_q, rope=latent),
              outs=[(2 * Q_PAIR, Q_DIM, BF16)], extra=extra, extra_specs=extra_specs, row_split=4,
              name="proj_q")
    z_ckv, z_kr, lora = proj(p['w_small'], tm=1024, tn=SMALL_COLS, epilogue=_epi_small,
                             outs=[(KV_RANK, KV_RANK, F32), (ROPE_DIM, ROPE_DIM, F32), (384, 384, F32)],
                             name="proj_small")
    z_rkv, = proj(p['w_rkv'], tm=1024, tn=1024, epilogue=_epi_plain,
                  outs=[(1024, 3 * R_DIM, F32)], name="proj_rkv")
    gates, = proj(p['w_gate'], tm=1024, tn=1024, epilogue=_epi_sigmoid,
                  outs=[(1024, 2 * D_MODEL, BF16)], name="proj_gate")

    if latent:
        ckv_ctx, kr_ctx, s0 = cache
        S = T + ckv_ctx.shape[1]
        ckv_all = jnp.concatenate([z_ckv.reshape(B, T, KV_RANK), ckv_ctx], axis=1).reshape(B * S, KV_RANK)
        kr_all = jnp.concatenate([z_kr.reshape(B, T, ROPE_DIM), kr_ctx], axis=1).reshape(B * S, ROPE_DIM)
        k, v = _keys_values(ckv_all, kr_all, p, (cos, sin), tm=S)
    else:
        S = T
        s0 = None
        k, v = _keys_values(z_ckv, z_kr, p, None, tm=1024)
    o_mla = _attention(q.reshape(B, T, Q_DIM), k, v, S, pairs=H_MLA // 2).reshape(M, MLA_WIDTH)

    yb, s_final = _rwkv_scan(z_rkv.reshape(B, T, 3 * R_DIM), lora.reshape(B, T, 384), p, s0)
    o_rwkv = _rwkv_post(*[a.reshape(M, R_DIM) for a in yb], lora, p)

    merged = _branch_merge(o_mla, o_rwkv, gates, p['w_br_mla'], p['w_br_rwkv'])
    x1 = _out_proj(merged, p['w_out'], xf, gate1, T)
    x2 = _ffn(x1, p['norm2'], scale2, shift2, gate2, p['w_ff_in'], p['w_ff_out'], T)
    return (x2.reshape(B, T, D_MODEL),
            (z_ckv.reshape(B, T, KV_RANK), z_kr.reshape(B, T, ROPE_DIM), s_final))


def _split_w_in(w):
    offs = [sum(IN_SIZES[:i]) for i in range(len(IN_SIZES))]
    part = lambda i: w[:, offs[i]:offs[i] + IN_SIZES[i]]
    K = w.shape[0]
    wq = part(0).reshape(K, H_MLA // 2, 2, QK_HEAD)
    wq = jnp.concatenate([wq[..., :NOPE_DIM].reshape(K, H_MLA // 2, 2 * NOPE_DIM),
                          wq[..., NOPE_DIM:].reshape(K, H_MLA // 2, 2 * ROPE_DIM)], axis=-1)
    wd, ad = part(4), part(5)
    small = jnp.concatenate([part(1), part(2), jnp.zeros((K, 128 - ROPE_DIM), w.dtype),
                             wd[:, :W_LORA], ad[:, :A_LORA], wd[:, W_LORA:], ad[:, A_LORA:],
                             part(6)], axis=1)
    return (wq.reshape(K, Q_DIM).astype(BF16), small.astype(BF16), part(3).astype(BF16),
            part(7).astype(BF16))


def _layer_params(l, w_in, q_norm, w_kv_up, w_br_mla, w_br_rwkv, w_out, w_ff_in, w_ff_out, **small):
    zeros = jnp.zeros((N_DIR, W_LORA, R_DIM), F32)
    w_q, w_small, w_rkv, w_gate = _split_w_in(w_in[l])
    qg = q_norm[l] * (QK_HEAD ** -0.5)
    q_gain = jnp.concatenate([qg[:NOPE_DIM], qg[:NOPE_DIM], qg[NOPE_DIM:], qg[NOPE_DIM:]])
    p = {name: val[l] for name, val in small.items()}
    p.update({
        'w_q': w_q, 'w_small': w_small, 'w_rkv': w_rkv, 'w_gate': w_gate,
        'q_gain': jnp.tile(q_gain, 2).reshape(1, 2 * Q_PAIR),
        'w_kv_up': w_kv_up[l].astype(BF16),
        'w_up_pad': jnp.concatenate([small['w_up'][l], zeros], axis=1).astype(BF16),
        'a_up_pad': jnp.concatenate([zeros, small['a_up'][l]], axis=1).astype(BF16),
        'g_up_bf16': small['g_up'][l].astype(BF16),
        'w_br_mla': w_br_mla[l].astype(BF16), 'w_br_rwkv': w_br_rwkv[l].astype(BF16),
        'w_out': w_out[l].astype(BF16),
        'w_ff_in': w_ff_in[l].astype(BF16), 'w_ff_out': w_ff_out[l].astype(BF16),
    })
    return p


def kernel(x_prompt, x_sample, cache_mla_ckv, cache_mla_kr, state_rwkv, c, c_ctx,
           norm1, w_ada, b_ada, w_in, q_norm, kv_norm, w_kv_up, k_norm, conv_rkv,
           k_k, k_a, r_k, w0, w_up, a0, a_up, g_up, lnx_w, lnx_b,
           w_br_mla, w_br_rwkv, w_out, norm2, w_ff_in, w_ff_out):
    x_p, x_s = x_prompt, x_sample
    n_lat = c.shape[0]
    ckv_list, kr_list, st_list = [], [], []
    for l in range(DEPTH):
        p = _layer_params(l, w_in, q_norm, w_kv_up, w_br_mla, w_br_rwkv, w_out, w_ff_in, w_ff_out,
                          norm1=norm1, kv_norm=kv_norm, k_norm=k_norm, conv_rkv=conv_rkv, k_k=k_k,
                          k_a=k_a, r_k=r_k, w0=w0, w_up=w_up, a0=a0, a_up=a_up, g_up=g_up,
                          lnx_w=lnx_w, lnx_b=lnx_b, norm2=norm2)
        cond = jnp.concatenate([c, c_ctx[None]], axis=0)
        cond = jnp.pad(jax.nn.silu(cond), ((0, (-cond.shape[0]) % 16), (0, 0)))
        mod = _matmul(cond, w_ada[l], tm=16, tn=1024)[:n_lat + 1] + b_ada[l]
        x_p, (ckv_l, kr_l, st_l) = _trunk_layer(x_p, mod[n_lat:], p, None)
        ckv_list.append(ckv_l)
        kr_list.append(kr_l)
        st_list.append(st_l)
        x_s, _ = _trunk_layer(x_s, mod[:n_lat], p,
                              (cache_mla_ckv[:, l], cache_mla_kr[:, l], state_rwkv[:, l]))
    return (x_p, x_s, jnp.stack(ckv_list, axis=1), jnp.stack(kr_list, axis=1),
            jnp.stack(st_list, axis=1))
```
